```python
import jax, jax.numpy as jnp
from jax import lax
import numpy as np

D_MODEL = 1024
BATCH = 8
SEQ = 8192
DEPTH = 1
DEC_BATCH = 16
DEC_SEQ = 2048
PAST_LEN = 128

DN_HEADS = 4
DN_DK = 128
DN_DV = 128
DN_QK = DN_HEADS * DN_DK
DN_V = DN_HEADS * DN_DV
DN_CHUNK = 64
CONV_K = 5
DN_CONV_CH = 2 * DN_QK + DN_V
SG_GROUPS = 4
SG_GROUP_DIM = 128
SG_WIDTH = SG_GROUPS * SG_GROUP_DIM
SG_CHUNK = 128
N_MEM = 256
XA_HEADS = 4
XA_HEAD_DIM = D_MODEL // XA_HEADS
D_FF = ((8 * D_MODEL + 3 * 256 - 1) // (3 * 256)) * 256
N_IN = DN_CONV_CH + 4 * DN_HEADS + DN_V + 2 * SG_WIDTH + 2 * D_MODEL
EPS = 1e-6

kernel_name = 'hybrid_deltanet_gmlp_memxattn_encoder'


def rmsnorm(x, w):
    xf = x.astype(jnp.float32)
    y = xf * lax.rsqrt(jnp.mean(xf * xf, -1, keepdims=True) + EPS)
    return (y * w.astype(jnp.float32)).astype(x.dtype)


def layernorm(x, w, b):
    xf = x.astype(jnp.float32)
    mu = jnp.mean(xf, -1, keepdims=True)
    xc = xf - mu
    y = xc * lax.rsqrt(jnp.mean(xc * xc, -1, keepdims=True) + EPS)
    return (y * w.astype(jnp.float32) + b.astype(jnp.float32)).astype(x.dtype)


def l2norm(x):
    xf = x.astype(jnp.float32)
    return (xf * lax.rsqrt(jnp.sum(xf * xf, -1, keepdims=True) + EPS)).astype(x.dtype)


def depthwise_conv(x, w):
    c = x.shape[-1]
    return lax.conv_general_dilated(
        x, w[:, None, :].astype(x.dtype), window_strides=(1,),
        padding=[(CONV_K // 2, CONV_K // 2)],
        dimension_numbers=('NWC', 'WIO', 'NWC'), feature_group_count=c)


def gated_delta_chunked(q, k, v, g, beta):
    out_dtype = v.dtype
    bsz, nh, seq, dk = q.shape
    dv = v.shape[-1]
    c = DN_CHUNK
    n = seq // c
    q = q.astype(jnp.float32).reshape(bsz, nh, n, c, dk) * (dk ** -0.5)
    k = k.astype(jnp.float32).reshape(bsz, nh, n, c, dk)
    v = v.astype(jnp.float32).reshape(bsz, nh, n, c, dv)
    g = jnp.cumsum(g.astype(jnp.float32).reshape(bsz, nh, n, c), -1)
    beta = beta.astype(jnp.float32).reshape(bsz, nh, n, c)
    incl = jnp.tril(jnp.ones((c, c), dtype=bool))
    strict = jnp.tril(jnp.ones((c, c), dtype=bool), -1)
    diff = g[..., :, None] - g[..., None, :]
    decay = jnp.where(incl, jnp.exp(jnp.where(incl, diff, 0.0)), 0.0)
    kb = k * beta[..., None]
    lower = jnp.where(strict, jnp.einsum('bhncd,bhnsd->bhncs', kb, k) * decay, 0.0)
    eye = jnp.eye(c, dtype=jnp.float32)
    tinv = lax.linalg.triangular_solve(lower + eye, jnp.broadcast_to(eye, lower.shape),
                                       left_side=True, lower=True, unit_diagonal=True)
    u = jnp.einsum('bhncs,bhnsd->bhncd', tinv, v * beta[..., None])
    w = jnp.einsum('bhncs,bhnsd->bhncd', tinv, kb * jnp.exp(g)[..., None])
    a_intra = jnp.einsum('bhncd,bhnsd->bhncs', q, k) * decay
    g_last = g[..., -1]
    qg = q * jnp.exp(g)[..., None]
    kg = k * jnp.exp(g_last[..., None] - g)[..., None]

    def step(state, xs):
        qg_i, kg_i, u_i, w_i, a_i, gl_i = xs
        v_new = u_i - jnp.einsum('bhcd,bhde->bhce', w_i, state)
        o = jnp.einsum('bhcd,bhde->bhce', qg_i, state) + jnp.einsum('bhcs,bhse->bhce', a_i, v_new)
        state = state * jnp.exp(gl_i)[..., None, None] + jnp.einsum('bhcd,bhce->bhde', kg_i, v_new)
        return state, o

    xs = tuple(jnp.moveaxis(t, 2, 0) for t in (qg, kg, u, w, a_intra, g_last))
    s0 = jnp.zeros((bsz, nh, dk, dv), jnp.float32)
    _, o = lax.scan(step, s0, xs)
    o = jnp.moveaxis(o, 0, 2).reshape(bsz, nh, seq, dv)
    return o.astype(out_dtype)


def deltanet_branch(qkv_raw, ab, gate, conv_w, a_log, dt_bias, norm_w):
    bsz, seq, _ = qkv_raw.shape
    qkv = jax.nn.silu(depthwise_conv(qkv_raw, conv_w))
    q = qkv[..., :DN_QK]
    k = qkv[..., DN_QK:2 * DN_QK]
    v = qkv[..., 2 * DN_QK:]
    heads = lambda t, d: jnp.transpose(t.reshape(bsz, seq, DN_HEADS, d), (0, 2, 1, 3))
    q = l2norm(heads(q, DN_DK))
    k = l2norm(heads(k, DN_DK))
    v = heads(v, DN_DV)
    ab = jnp.transpose(ab.astype(jnp.float32).reshape(bsz, seq, 4, DN_HEADS), (2, 0, 3, 1))
    beta_f = jax.nn.sigmoid(ab[0])
    beta_b = jax.nn.sigmoid(ab[1])
    a_log = a_log.astype(jnp.float32)
    dt_bias = dt_bias.astype(jnp.float32)
    g_f = -jnp.exp(a_log[0])[None, :, None] * jax.nn.softplus(ab[2] + dt_bias[0][None, :, None])
    g_b = -jnp.exp(a_log[1])[None, :, None] * jax.nn.softplus(ab[3] + dt_bias[1][None, :, None])
    o_f = gated_delta_chunked(q, k, v, g_f, beta_f)
    flip = lambda t: jnp.flip(t, axis=2)
    o_b = flip(gated_delta_chunked(flip(q), flip(k), flip(v), flip(g_b), flip(beta_b)))
    o = jnp.transpose(o_f + o_b, (0, 2, 1, 3))
    o = rmsnorm(o, norm_w) * jax.nn.silu(gate.reshape(bsz, seq, DN_HEADS, DN_DV))
    return o.reshape(bsz, seq, DN_V)


def spatial_gating_branch(u, v, ln_w, ln_b, w_s, b_s):
    bsz, seq, _ = u.shape
    u = jax.nn.gelu(u)
    v = layernorm(jax.nn.gelu(v), ln_w, ln_b)
    v = v.reshape(bsz, seq // SG_CHUNK, SG_CHUNK, SG_GROUPS, SG_GROUP_DIM)
    mixed = jnp.einsum('gts,bnsgc->bntgc', w_s, v) + jnp.transpose(b_s)[None, None, :, :, None]
    return u * mixed.reshape(bsz, seq, SG_WIDTH)


def memory_cross_attention(h, mem, w_q, w_kv, w_o):
    bsz, seq, _ = h.shape
    q = (h @ w_q).reshape(bsz, seq, XA_HEADS, XA_HEAD_DIM)
    kv = (mem @ w_kv).reshape(bsz, mem.shape[1], 2, XA_HEADS, XA_HEAD_DIM)
    k = kv[:, :, 0]
    v = kv[:, :, 1]
    s = jnp.einsum('blhd,bmhd->bhlm', q, k).astype(jnp.float32) * (XA_HEAD_DIM ** -0.5)
    p = jax.nn.softmax(s, axis=-1).astype(v.dtype)
    o = jnp.einsum('bhlm,bmhd->blhd', p, v).reshape(bsz, seq, D_MODEL)
    return o @ w_o


def encoder_layer(x, mem, norm_mix_w, w_in, conv_w, dn_a_log, dn_dt_bias, dn_norm_w, w_up_a,
                  sg_ln_w, sg_ln_b, sg_w, sg_b, w_up_b, w_out, norm_xa_w, norm_mem_w,
                  xa_w_q, xa_w_kv, xa_w_o, norm_ffn_w, ffn_w_gate_up, ffn_w_down):
    h = rmsnorm(x, norm_mix_w)
    proj = h @ w_in
    sizes = [DN_CONV_CH, 4 * DN_HEADS, DN_V, SG_WIDTH, SG_WIDTH, D_MODEL]
    qkv_raw, ab, dn_gate, sg_u, sg_v, gate_a, gate_b = jnp.split(proj, np.cumsum(sizes).tolist(), axis=-1)
    y_a = deltanet_branch(qkv_raw, ab, dn_gate, conv_w, dn_a_log, dn_dt_bias, dn_norm_w) @ w_up_a
    y_b = spatial_gating_branch(sg_u, sg_v, sg_ln_w, sg_ln_b, sg_w, sg_b) @ w_up_b
    merged = jax.nn.sigmoid(gate_a) * y_a + jax.nn.sigmoid(gate_b) * y_b
    x = x + merged @ w_out
    x = x + memory_cross_attention(rmsnorm(x, norm_xa_w), rmsnorm(mem, norm_mem_w), xa_w_q, xa_w_kv, xa_w_o)
    h = rmsnorm(x, norm_ffn_w)
    gu = h @ ffn_w_gate_up
    x = x + (jax.nn.silu(gu[..., :D_FF]) * gu[..., D_FF:]) @ ffn_w_down
    return x


def setup_inputs(seed: int = 0) -> dict:
    key = jax.random.key(seed)
    ks = iter(jax.random.split(key, 48))
    f32 = jnp.float32
    nrm = lambda shape, fan: jax.random.normal(next(ks), shape, f32) * (fan ** -0.5)
    gain = lambda shape: 1.0 + 0.02 * jax.random.normal(next(ks), shape, f32)
    small = lambda shape: 0.02 * jax.random.normal(next(ks), shape, f32)
    nl = DEPTH
    a_log = jnp.log(jax.random.uniform(next(ks), (nl, 2, DN_HEADS), f32, 1.0, 16.0))
    dt = jnp.exp(jax.random.uniform(next(ks), (nl, 2, DN_HEADS), f32, np.log(1e-3), np.log(1e-1)))
    dt_bias = dt + jnp.log(-jnp.expm1(-dt))
    return {
        'x_prompt': jax.random.normal(next(ks), (BATCH, SEQ, D_MODEL), f32),
        'x_sample': jax.random.normal(next(ks), (DEC_BATCH, DEC_SEQ, D_MODEL), f32),
        'mem_prompt': jax.random.normal(next(ks), (BATCH, N_MEM, D_MODEL), f32),
        'mem_sample': jax.random.normal(next(ks), (DEC_BATCH, N_MEM, D_MODEL), f32),
        'norm_mix_w': gain((nl, D_MODEL)),
        'w_in': nrm((nl, D_MODEL, N_IN), D_MODEL),
        'conv_w': nrm((nl, CONV_K, DN_CONV_CH), CONV_K),
        'dn_a_log': a_log,
        'dn_dt_bias': dt_bias,
        'dn_norm_w': gain((nl, DN_DV)),
        'w_up_a': nrm((nl, DN_V, D_MODEL), DN_V),
        'sg_ln_w': gain((nl, SG_WIDTH)),
        'sg_ln_b': small((nl, SG_WIDTH)),
        'sg_w': nrm((nl, SG_GROUPS, SG_CHUNK, SG_CHUNK), SG_CHUNK),
        'sg_b': small((nl, SG_GROUPS, SG_CHUNK)),
        'w_up_b': nrm((nl, SG_WIDTH, D_MODEL), SG_WIDTH),
        'w_out': nrm((nl, D_MODEL, D_MODEL), D_MODEL),
        'norm_xa_w': gain((nl, D_MODEL)),
        'norm_mem_w': gain((nl, D_MODEL)),
        'xa_w_q': nrm((nl, D_MODEL, D_MODEL), D_MODEL),
        'xa_w_kv': nrm((nl, D_MODEL, 2 * D_MODEL), D_MODEL),
        'xa_w_o': nrm((nl, D_MODEL, D_MODEL), D_MODEL),
        'norm_ffn_w': gain((nl, D_MODEL)),
        'ffn_w_gate_up': nrm((nl, D_MODEL, 2 * D_FF), D_MODEL),
        'ffn_w_down': nrm((nl, D_FF, D_MODEL), D_FF),
        'final_norm_w': gain((D_MODEL,)),
    }


def reference(x_prompt, x_sample, mem_prompt, mem_sample, norm_mix_w, w_in, conv_w, dn_a_log,
              dn_dt_bias, dn_norm_w, w_up_a, sg_ln_w, sg_ln_b, sg_w, sg_b, w_up_b, w_out,
              norm_xa_w, norm_mem_w, xa_w_q, xa_w_kv, xa_w_o, norm_ffn_w, ffn_w_gate_up,
              ffn_w_down, final_norm_w):
    def trunk(x, mem):
        for l in range(DEPTH):
            x = encoder_layer(x, mem, norm_mix_w[l], w_in[l], conv_w[l], dn_a_log[l], dn_dt_bias[l],
                              dn_norm_w[l], w_up_a[l], sg_ln_w[l], sg_ln_b[l], sg_w[l], sg_b[l],
                              w_up_b[l], w_out[l], norm_xa_w[l], norm_mem_w[l], xa_w_q[l],
                              xa_w_kv[l], xa_w_o[l], norm_ffn_w[l], ffn_w_gate_up[l], ffn_w_down[l])
        return rmsnorm(x, final_norm_w)

    y_prompt = trunk(x_prompt, mem_prompt)
    y_sample = trunk(x_sample, mem_sample)
    return (y_prompt, y_sample)
```

```python
import functools

import jax
import jax.numpy as jnp
from jax import lax
from jax.experimental import pallas as pl
from jax.experimental.pallas import tpu as pltpu

F32 = jnp.float32
BF16 = jnp.bfloat16
EPS = 1e-6

DN_HEADS = 4
DN_D = 128
DN_W = DN_HEADS * DN_D
DN_CHUNK = 64
CONV_K = 5
SG_GROUPS = 4
SG_CHUNK = 128
SG_W = SG_GROUPS * SG_CHUNK
XA_HEADS = 4

CONV_HALO_ROWS = 8
VMEM_LIMIT_BYTES = 56 * 1024 * 1024


def _dot(a, b):
    return jnp.dot(a.astype(BF16), b.astype(BF16), preferred_element_type=F32)


def _dot_nt(a, b):
    return lax.dot_general(a.astype(BF16), b.astype(BF16), (((1,), (1,)), ((), ())),
                           preferred_element_type=F32)


def _dot_tn(a, b):
    return lax.dot_general(a.astype(BF16), b.astype(BF16), (((0,), (0,)), ((), ())),
                           preferred_element_type=F32)


def _dot_f32(a, b):
    return jnp.dot(a, b, precision=lax.Precision.HIGHEST, preferred_element_type=F32)


def _rms(x, w):
    return x * lax.rsqrt(jnp.mean(x * x, -1, keepdims=True) + EPS) * w


def _silu(x):
    return x * jax.nn.sigmoid(x)


def _gelu_tanh(x):
    c = 0.7978845608028654
    return x * (0.5 * (1.0 + jnp.tanh(c * (x + 0.044715 * (x * x * x)))))


def _params(*sem):
    return pltpu.CompilerParams(dimension_semantics=sem, vmem_limit_bytes=VMEM_LIMIT_BYTES)


def _const_spec(shape):
    nd = len(shape)
    return pl.BlockSpec(shape, lambda *_: (0,) * nd)


def _gate_math(ab, a_log, dt_bias, is_beta):
    z = ab + dt_bias
    softplus = jnp.maximum(z, 0.0) + jnp.log1p(jnp.exp(-jnp.abs(z)))
    return jnp.where(is_beta, jax.nn.sigmoid(ab), -jnp.exp(a_log) * softplus)


def _inproj_kernel(x_ref, nw_ref, wqkv_ref, wdng_ref, wsgu_ref, wsgv_ref, wga_ref, wgb_ref,
                   wab_ref, wabt_ref, alog_r_ref, dtb_r_ref, alog_c_ref, dtb_c_ref,
                   qkv_ref, dng_ref, sgu_ref, sgv_ref, ga_ref, gb_ref, gate_ref, gatet_ref):
    h = _rms(x_ref[...], nw_ref[...]).astype(BF16)
    for w_ref, o_ref in ((wqkv_ref, qkv_ref), (wdng_ref, dng_ref), (wsgu_ref, sgu_ref),
                         (wsgv_ref, sgv_ref), (wga_ref, ga_ref), (wgb_ref, gb_ref)):
        n = w_ref.shape[1]
        for j in range(0, n, 512):
            o_ref[:, j:j + 512] = _dot(h, w_ref[:, j:j + 512]).astype(o_ref.dtype)
    ab = _dot(h, wab_ref[...])
    col = lax.broadcasted_iota(jnp.int32, ab.shape, 1)
    gate_ref[...] = _gate_math(ab, alog_r_ref[...], dtb_r_ref[...], col < 2 * DN_HEADS)
    abt = _dot_nt(wabt_ref[...], h)
    row = lax.broadcasted_iota(jnp.int32, abt.shape, 0)
    gatet_ref[...] = _gate_math(abt, alog_c_ref[...], dtb_c_ref[...], row < 2 * DN_HEADS)


def _inproj(x2d, norm_w, w_in, a_log, dt_bias, tm):
    t, d = x2d.shape
    n_qkv = 3 * DN_W
    n_ab = 4 * DN_HEADS
    bounds = [0, n_qkv, n_qkv + n_ab]
    for width in (DN_W, SG_W, SG_W, d, d):
        bounds.append(bounds[-1] + width)
    assert bounds[-1] == w_in.shape[1]
    piece = lambda k: w_in[:, bounds[k]:bounds[k + 1]].astype(BF16)
    w_qkv, w_ab = piece(0), piece(1)
    w_rest = [piece(k) for k in range(2, 7)]
    zeros8 = jnp.zeros((2 * DN_HEADS,), F32)
    alog16 = jnp.concatenate([zeros8, a_log.reshape(-1).astype(F32)])
    dtb16 = jnp.concatenate([zeros8, dt_bias.reshape(-1).astype(F32)])
    widths = [n_qkv, DN_W, SG_W, SG_W, d, d]
    out_shape = [jax.ShapeDtypeStruct((t, n), BF16) for n in widths]
    out_shape += [jax.ShapeDtypeStruct((t, n_ab), F32), jax.ShapeDtypeStruct((n_ab, t), F32)]
    out_specs = [pl.BlockSpec((tm, n), lambda i: (i, 0)) for n in widths]
    out_specs += [pl.BlockSpec((tm, n_ab), lambda i: (i, 0)), pl.BlockSpec((n_ab, tm), lambda i: (0, i))]
    consts = [norm_w.reshape(1, d).astype(F32), w_qkv, *w_rest, w_ab, w_ab.T,
              alog16.reshape(1, n_ab), dtb16.reshape(1, n_ab), alog16.reshape(n_ab, 1), dtb16.reshape(n_ab, 1)]
    return pl.pallas_call(
        _inproj_kernel,
        grid=(t // tm,),
        in_specs=[pl.BlockSpec((tm, d), lambda i: (i, 0))] + [_const_spec(c.shape) for c in consts],
        out_specs=out_specs,
        out_shape=out_shape,
        compiler_params=_params("parallel"),
        name="inproj",
    )(x2d, *consts)


def _unit_triangular_inverse(lm, eye):
    x = eye - lm
    p = _dot(lm, lm)
    power = 2
    while 2 * power < DN_CHUNK:
        x = x + _dot(x, p)
        p = _dot(p, p)
        power *= 2
    return x + _dot(x, p)


def _delta_kernel(prev_ref, cur_ref, next_ref, gate_ref, gatet_ref, cw_ref, o_ref,
                  state_ref, raw_ref, qkv_s, *, rev, cb, nb):
    i = pl.program_id(1)
    ieff = (nb - 1 - i) if rev else i
    c = DN_CHUNK
    h0 = CONV_HALO_ROWS

    @pl.when(i == 0)
    def _():
        state_ref[...] = jnp.zeros_like(state_ref)

    raw_ref[0:h0, :] = prev_ref[...].astype(F32) * jnp.where(ieff > 0, 1.0, 0.0)
    raw_ref[h0:h0 + cb, :] = cur_ref[...].astype(F32)
    raw_ref[h0 + cb:2 * h0 + cb, :] = next_ref[...].astype(F32) * jnp.where(ieff < nb - 1, 1.0, 0.0)

    pad = CONV_K // 2
    for g in range(3 * DN_HEADS):
        cols = slice(g * DN_D, (g + 1) * DN_D)
        acc = raw_ref[h0 - pad:h0 - pad + cb, cols] * cw_ref[0:1, cols]
        for j in range(1, CONV_K):
            acc = acc + raw_ref[h0 - pad + j:h0 - pad + j + cb, cols] * cw_ref[j:j + 1, cols]
        y = _silu(acc)
        if g < 2 * DN_HEADS:
            y = y * lax.rsqrt(jnp.sum(y * y, -1, keepdims=True) + EPS)
        if g < DN_HEADS:
            y = y * (DN_D ** -0.5)
        qkv_s[:, cols] = y

    ri = lax.broadcasted_iota(jnp.int32, (cb, cb), 0)
    ci = lax.broadcasted_iota(jnp.int32, (cb, cb), 1)
    same = (ri & -c) == (ci & -c)
    tri = (ri <= ci) if rev else (ri >= ci)
    tri_t = (ri >= ci) if rev else (ri <= ci)
    gates = gate_ref[...]
    gates_t = gatet_ref[...]
    gc = _dot_f32(jnp.where(same & tri, 1.0, 0.0), gates)
    gc_t = _dot_f32(gates_t, jnp.where(same & tri_t, 1.0, 0.0))

    r64 = lax.broadcasted_iota(jnp.int32, (c, c), 0)
    c64 = lax.broadcasted_iota(jnp.int32, (c, c), 1)
    incl = (r64 <= c64) if rev else (r64 >= c64)
    strict = (r64 < c64) if rev else (r64 > c64)
    eye = jnp.where(r64 == c64, 1.0, 0.0)
    beta_col0 = DN_HEADS if rev else 0
    g_col0 = (3 if rev else 2) * DN_HEADS

    chunk_order = range(cb // c - 1, -1, -1) if rev else range(cb // c)
    for n in chunk_order:
        rows = slice(n * c, (n + 1) * c)
        for h in range(DN_HEADS):
            q = qkv_s[rows, h * DN_D:(h + 1) * DN_D]
            k = qkv_s[rows, DN_W + h * DN_D:DN_W + (h + 1) * DN_D]
            v = qkv_s[rows, 2 * DN_W + h * DN_D:2 * DN_W + (h + 1) * DN_D]
            beta = gates[rows, beta_col0 + h:beta_col0 + h + 1]
            g_col = gc[rows, g_col0 + h:g_col0 + h + 1]
            g_row = gc_t[g_col0 + h:g_col0 + h + 1, rows]
            g_last = g_col[0:1, :] if rev else g_col[c - 1:c, :]
            decay = jnp.where(incl, jnp.exp(jnp.where(incl, g_col - g_row, 0.0)), 0.0)
            kb = k * beta
            lm = jnp.where(strict, _dot_nt(kb, k) * decay, 0.0)
            a_intra = _dot_nt(q, k) * decay
            tinv = _unit_triangular_inverse(lm, eye)
            eg = jnp.exp(g_col)
            uw = _dot(tinv, jnp.concatenate([v * beta, kb * eg], axis=1))
            state = state_ref[h]
            ws_qs = _dot(jnp.concatenate([uw[:, DN_D:], q * eg], axis=0), state)
            v_new = uw[:, :DN_D] - ws_qs[:c]
            o = ws_qs[c:] + _dot(a_intra, v_new)
            kg = k * jnp.exp(g_last - g_col)
            state_ref[h] = state * jnp.exp(g_last) + _dot_tn(kg, v_new)
            o_ref[rows, h * DN_D:(h + 1) * DN_D] = o.astype(o_ref.dtype)


def _delta_scan(qkv, gates, gates_t, conv_w, rev, cb):
    bsz, seq, n_qkv = qkv.shape
    nb = seq // cb
    hb = cb // CONV_HALO_ROWS
    last_halo = seq // CONV_HALO_ROWS - 1
    if rev:
        blk = lambda i: nb - 1 - i
    else:
        blk = lambda i: i
    return pl.pallas_call(
        functools.partial(_delta_kernel, rev=rev, cb=cb, nb=nb),
        grid=(bsz, nb),
        in_specs=[
            pl.BlockSpec((None, CONV_HALO_ROWS, n_qkv), lambda b, i: (b, jnp.maximum(blk(i) * hb - 1, 0), 0)),
            pl.BlockSpec((None, cb, n_qkv), lambda b, i: (b, blk(i), 0)),
            pl.BlockSpec((None, CONV_HALO_ROWS, n_qkv),
                         lambda b, i: (b, jnp.minimum((blk(i) + 1) * hb, last_halo), 0)),
            pl.BlockSpec((cb, 4 * DN_HEADS), lambda b, i: (b * nb + blk(i), 0)),
            pl.BlockSpec((4 * DN_HEADS, cb), lambda b, i: (0, b * nb + blk(i))),
            _const_spec(conv_w.shape),
        ],
        out_specs=pl.BlockSpec((None, cb, DN_W), lambda b, i: (b, blk(i), 0)),
        out_shape=jax.ShapeDtypeStruct((bsz, seq, DN_W), BF16),
        scratch_shapes=[
            pltpu.VMEM((DN_HEADS, DN_D, DN_D), F32),
            pltpu.VMEM((cb + 2 * CONV_HALO_ROWS, n_qkv), F32),
            pltpu.VMEM((cb, n_qkv), F32),
        ],
        compiler_params=_params("parallel", "arbitrary"),
        name="delta_bwd" if rev else "delta_fwd",
    )(qkv, qkv, qkv, gates, gates_t, conv_w)


def _mix_kernel(x_ref, of_ref, ob_ref, dng_ref, sgu_ref, sgv_ref, ga_ref, gb_ref,
                dnw_ref, wua_ref, lnw_ref, lnb_ref, ws_ref, bst_ref, wub_ref, wout_ref,
                y_ref, dn_s, sg_s):
    tm = x_ref.shape[0]
    for h in range(DN_HEADS):
        cols = slice(h * DN_D, (h + 1) * DN_D)
        o = of_ref[:, cols].astype(F32) + ob_ref[:, cols].astype(F32)
        dn_s[:, cols] = (_rms(o, dnw_ref[...]) * _silu(dng_ref[:, cols].astype(F32))).astype(BF16)
    y_a = _dot(dn_s[...], wua_ref[...])
    v = _gelu_tanh(sgv_ref[...].astype(F32))
    mu = jnp.mean(v, -1, keepdims=True)
    vc = v - mu
    v = vc * lax.rsqrt(jnp.mean(vc * vc, -1, keepdims=True) + EPS) * lnw_ref[...] + lnb_ref[...]
    v = v.astype(BF16)
    for n in range(tm // SG_CHUNK):
        rows = slice(n * SG_CHUNK, (n + 1) * SG_CHUNK)
        for g in range(SG_GROUPS):
            cols = slice(g * SG_CHUNK, (g + 1) * SG_CHUNK)
            mixed = _dot(ws_ref[g], v[rows, cols]) + bst_ref[:, g:g + 1]
            sg_s[rows, cols] = (_gelu_tanh(sgu_ref[rows, cols].astype(F32)) * mixed).astype(BF16)
    y_b = _dot(sg_s[...], wub_ref[...])
    merged = jax.nn.sigmoid(ga_ref[...].astype(F32)) * y_a + jax.nn.sigmoid(gb_ref[...].astype(F32)) * y_b
    y_ref[...] = x_ref[...] + _dot(merged, wout_ref[...])


def _mix(x2d, o_f, o_b, dng, sgu, sgv, ga, gb, dn_norm_w, w_up_a, ln_w, ln_b, sg_w, sg_b, w_up_b, w_out, tm):
    t, d = x2d.shape
    consts = [dn_norm_w.reshape(1, DN_D).astype(F32), w_up_a.astype(BF16),
              ln_w.reshape(1, SG_W).astype(F32), ln_b.reshape(1, SG_W).astype(F32),
              sg_w.astype(BF16), sg_b.T.astype(F32), w_up_b.astype(BF16), w_out.astype(BF16)]
    acts = [x2d, o_f, o_b, dng, sgu, sgv, ga, gb]
    return pl.pallas_call(
        _mix_kernel,
        grid=(t // tm,),
        in_specs=[pl.BlockSpec((tm, a.shape[1]), lambda i: (i, 0)) for a in acts]
                 + [_const_spec(c.shape) for c in consts],
        out_specs=pl.BlockSpec((tm, d), lambda i: (i, 0)),
        out_shape=jax.ShapeDtypeStruct((t, d), F32),
        scratch_shapes=[pltpu.VMEM((tm, DN_W), BF16), pltpu.VMEM((tm, SG_W), BF16)],
        compiler_params=_params("parallel"),
        name="mix",
    )(*acts, *consts)


def _norm_proj_kernel(x_ref, nw_ref, w_ref, o_ref):
    o_ref[...] = _dot(_rms(x_ref[...], nw_ref[...]), w_ref[...]).astype(o_ref.dtype)


def _norm_proj(x2d, norm_w, w, tm):
    t, d = x2d.shape
    n = w.shape[1]
    return pl.pallas_call(
        _norm_proj_kernel,
        grid=(t // tm,),
        in_specs=[pl.BlockSpec((tm, d), lambda i: (i, 0)), _const_spec((1, d)), _const_spec(w.shape)],
        out_specs=pl.BlockSpec((tm, n), lambda i: (i, 0)),
        out_shape=jax.ShapeDtypeStruct((t, n), BF16),
        compiler_params=_params("parallel"),
        name="mem_kv",
    )(x2d, norm_w.reshape(1, d).astype(F32), w.astype(BF16))


def _xattn_kernel(x_ref, kv_ref, nw_ref, wq_ref, wo_ref, y_ref, o_s):
    d = x_ref.shape[1]
    hd = d // XA_HEADS
    x = x_ref[...]
    q = _dot(_rms(x, nw_ref[...]), wq_ref[...])
    for h in range(XA_HEADS):
        cols = slice(h * hd, (h + 1) * hd)
        s = _dot_nt(q[:, cols], kv_ref[:, cols]) * (hd ** -0.5)
        p = jnp.exp(s - jnp.max(s, -1, keepdims=True))
        p = p / jnp.sum(p, -1, keepdims=True)
        o_s[:, cols] = _dot(p, kv_ref[:, d + h * hd:d + (h + 1) * hd]).astype(BF16)
    y_ref[...] = x + _dot(o_s[...], wo_ref[...])


def _xattn(x3d, kv, norm_w, w_q, w_o, tm):
    bsz, seq, d = x3d.shape
    n_mem = kv.shape[1]
    consts = [norm_w.reshape(1, d).astype(F32), w_q.astype(BF16), w_o.astype(BF16)]
    return pl.pallas_call(
        _xattn_kernel,
        grid=(bsz, seq // tm),
        in_specs=[pl.BlockSpec((None, tm, d), lambda b, i: (b, i, 0)),
                  pl.BlockSpec((None, n_mem, 2 * d), lambda b, i: (b, 0, 0))]
                 + [_const_spec(c.shape) for c in consts],
        out_specs=pl.BlockSpec((None, tm, d), lambda b, i: (b, i, 0)),
        out_shape=jax.ShapeDtypeStruct((bsz, seq, d), F32),
        scratch_shapes=[pltpu.VMEM((tm, d), BF16)],
        compiler_params=_params("parallel", "parallel"),
        name="xattn",
    )(x3d, kv, *consts)


def _ffn_kernel(x_ref, nw_ref, wg_ref, wu_ref, wd_ref, fw_ref, y_ref, *, ff_chunks):
    x = x_ref[...]
    h = _rms(x, nw_ref[...]).astype(BF16)
    acc = x
    for lo, hi in ff_chunks:
        a = _silu(_dot(h, wg_ref[:, lo:hi])) * _dot(h, wu_ref[:, lo:hi])
        acc = acc + _dot(a, wd_ref[lo:hi, :])
    y_ref[...] = _rms(acc, fw_ref[...])


def _ffn(x2d, norm_w, w_gate_up, w_down, final_w, tm):
    t, d = x2d.shape
    d_ff = w_down.shape[0]
    mxu_cols = 256
    assert d_ff % mxu_cols == 0
    half = (d_ff // mxu_cols + 1) // 2 * mxu_cols
    ff_chunks = ((0, half), (half, d_ff))
    consts = [norm_w.reshape(1, d).astype(F32), w_gate_up[:, :d_ff].astype(BF16),
              w_gate_up[:, d_ff:].astype(BF16), w_down.astype(BF16), final_w.reshape(1, d).astype(F32)]
    return pl.pallas_call(
        functools.partial(_ffn_kernel, ff_chunks=ff_chunks),
        grid=(t // tm,),
        in_specs=[pl.BlockSpec((tm, d), lambda i: (i, 0))] + [_const_spec(c.shape) for c in consts],
        out_specs=pl.BlockSpec((tm, d), lambda i: (i, 0)),
        out_shape=jax.ShapeDtypeStruct((t, d), F32),
        compiler_params=_params("parallel"),
        name="ffn",
    )(x2d, *consts)


def _largest_tile(n, cap):
    t = cap
    while n % t:
        t //= 2
    return t


def _trunk(x, mem, norm_mix_w, w_in, conv_w, dn_a_log, dn_dt_bias, dn_norm_w, w_up_a, sg_ln_w, sg_ln_b,
           sg_w, sg_b, w_up_b, w_out, norm_xa_w, norm_mem_w, xa_w_q, xa_w_kv, xa_w_o, norm_ffn_w,
           ffn_w_gate_up, ffn_w_down, final_norm_w):
    bsz, seq, d = x.shape
    t = bsz * seq
    assert seq % SG_CHUNK == 0 and seq % DN_CHUNK == 0
    tm = _largest_tile(seq, 512)
    cb = _largest_tile(seq, 256)
    x2d = x.reshape(t, d)
    qkv, dng, sgu, sgv, ga, gb, gates, gates_t = _inproj(x2d, norm_mix_w, w_in, dn_a_log, dn_dt_bias, tm)
    qkv3 = qkv.reshape(bsz, seq, -1)
    cw = conv_w.astype(F32)
    o_f = _delta_scan(qkv3, gates, gates_t, cw, False, cb).reshape(t, DN_W)
    o_b = _delta_scan(qkv3, gates, gates_t, cw, True, cb).reshape(t, DN_W)
    x2d = _mix(x2d, o_f, o_b, dng, sgu, sgv, ga, gb, dn_norm_w, w_up_a, sg_ln_w, sg_ln_b, sg_w, sg_b,
               w_up_b, w_out, tm)
    n_mem = mem.shape[1]
    kv = _norm_proj(mem.reshape(bsz * n_mem, d), norm_mem_w, xa_w_kv, _largest_tile(bsz * n_mem, 256))
    x3d = _xattn(x2d.reshape(bsz, seq, d), kv.reshape(bsz, n_mem, 2 * d), norm_xa_w, xa_w_q, xa_w_o, tm)
    y = _ffn(x3d.reshape(t, d), norm_ffn_w, ffn_w_gate_up, ffn_w_down, final_norm_w, _largest_tile(seq, 256))
    return y.reshape(bsz, seq, d)


def kernel(x_prompt, x_sample, mem_prompt, mem_sample, norm_mix_w, w_in, conv_w, dn_a_log, dn_dt_bias, dn_norm_w, w_up_a, sg_ln_w, sg_ln_b, sg_w, sg_b, w_up_b, w_out, norm_xa_w, norm_mem_w, xa_w_q, xa_w_kv, xa_w_o, norm_ffn_w, ffn_w_gate_up, ffn_w_down, final_norm_w):
    depth = w_in.shape[0]
    assert depth == 1, "the FFN kernel fuses the final norm, which assumes a single layer"
    layer = (norm_mix_w[0], w_in[0], conv_w[0], dn_a_log[0], dn_dt_bias[0], dn_norm_w[0], w_up_a[0],
             sg_ln_w[0], sg_ln_b[0], sg_w[0], sg_b[0], w_up_b[0], w_out[0], norm_xa_w[0], norm_mem_w[0],
             xa_w_q[0], xa_w_kv[0], xa_w_o[0], norm_ffn_w[0], ffn_w_gate_up[0], ffn_w_down[0], final_norm_w)
    y_prompt = _trunk(x_prompt, mem_prompt, *layer)
    y_sample = _trunk(x_sample, mem_sample, *layer)
    return (y_prompt, y_sample)
```

```python
import functools

import jax
import jax.numpy as jnp
from jax import lax
from jax.experimental import pallas as pl
from jax.experimental.pallas import tpu as pltpu

F32 = jnp.float32
BF16 = jnp.bfloat16
EPS = 1e-6

DN_HEADS = 4
DN_D = 128
DN_W = DN_HEADS * DN_D
DN_CHUNK = 64
CONV_K = 5
SG_GROUPS = 4
SG_CHUNK = 128
SG_W = SG_GROUPS * SG_CHUNK
XA_HEADS = 4

CONV_HALO_ROWS = 8
VMEM_LIMIT_BYTES = 56 * 1024 * 1024


def _dot(a, b):
    return jnp.dot(a.astype(BF16), b.astype(BF16), preferred_element_type=F32)


def _dot_nt(a, b):
    return lax.dot_general(a.astype(BF16), b.astype(BF16), (((1,), (1,)), ((), ())),
                           preferred_element_type=F32)


def _dot_tn(a, b):
    return lax.dot_general(a.astype(BF16), b.astype(BF16), (((0,), (0,)), ((), ())),
                           preferred_element_type=F32)


def _dot_f32(a, b):
    return jnp.dot(a, b, precision=lax.Precision.HIGHEST, preferred_element_type=F32)


def _rms(x, w):
    return x * lax.rsqrt(jnp.mean(x * x, -1, keepdims=True) + EPS) * w


def _silu(x):
    return x * jax.nn.sigmoid(x)


def _gelu_tanh(x):
    c = 0.7978845608028654
    return x * (0.5 * (1.0 + jnp.tanh(c * (x + 0.044715 * (x * x * x)))))


def _params(*sem):
    return pltpu.CompilerParams(dimension_semantics=sem, vmem_limit_bytes=VMEM_LIMIT_BYTES)


def _const_spec(shape):
    nd = len(shape)
    return pl.BlockSpec(shape, lambda *_: (0,) * nd)


def _gate_math(ab, a_log, dt_bias, is_beta):
    z = ab + dt_bias
    softplus = jnp.maximum(z, 0.0) + jnp.log1p(jnp.exp(-jnp.abs(z)))
    return jnp.where(is_beta, jax.nn.sigmoid(ab), -jnp.exp(a_log) * softplus)


def _inproj_kernel(x_ref, xp_ref, xn_ref, nw_ref, wqkv_ref, cw_ref, wdng_ref, wsgu_ref, wsgv_ref, wga_ref, wgb_ref,
                   wab_ref, wabt_ref, alog_r_ref, dtb_r_ref, alog_c_ref, dtb_c_ref,
                   qkv_ref, dng_ref, sgu_ref, sgv_ref, ga_ref, gb_ref, gate_ref, gatet_ref, raw_ref,
                   *, tiles_per_seq):
    tm = x_ref.shape[0]
    h0 = CONV_HALO_ROWS
    pad = CONV_K // 2
    pos = pl.program_id(0) % tiles_per_seq
    h = _rms(x_ref[...], nw_ref[...]).astype(BF16)
    raw_ref[0:h0, :] = _dot(_rms(xp_ref[...], nw_ref[...]), wqkv_ref[...]) * jnp.where(pos > 0, 1.0, 0.0)
    raw_ref[h0 + tm:2 * h0 + tm, :] = (_dot(_rms(xn_ref[...], nw_ref[...]), wqkv_ref[...])
                                       * jnp.where(pos < tiles_per_seq - 1, 1.0, 0.0))
    for j in range(0, 3 * DN_W, 512):
        raw_ref[h0:h0 + tm, j:j + 512] = _dot(h, wqkv_ref[:, j:j + 512])
    for g in range(3 * DN_HEADS):
        cols = slice(g * DN_D, (g + 1) * DN_D)
        acc = raw_ref[h0 - pad:h0 - pad + tm, cols] * cw_ref[0:1, cols]
        for j in range(1, CONV_K):
            acc = acc + raw_ref[h0 - pad + j:h0 - pad + j + tm, cols] * cw_ref[j:j + 1, cols]
        y = _silu(acc)
        if g < 2 * DN_HEADS:
            y = y * lax.rsqrt(jnp.sum(y * y, -1, keepdims=True) + EPS)
        if g < DN_HEADS:
            y = y * (DN_D ** -0.5)
        qkv_ref[:, cols] = y.astype(BF16)
    for w_ref, o_ref in ((wdng_ref, dng_ref), (wsgu_ref, sgu_ref),
                         (wsgv_ref, sgv_ref), (wga_ref, ga_ref), (wgb_ref, gb_ref)):
        n = w_ref.shape[1]
        for j in range(0, n, 512):
            o_ref[:, j:j + 512] = _dot(h, w_ref[:, j:j + 512]).astype(o_ref.dtype)
    ab = _dot(h, wab_ref[...])
    col = lax.broadcasted_iota(jnp.int32, ab.shape, 1)
    gate_ref[...] = _gate_math(ab, alog_r_ref[...], dtb_r_ref[...], col < 2 * DN_HEADS)
    abt = _dot_nt(wabt_ref[...], h)
    row = lax.broadcasted_iota(jnp.int32, abt.shape, 0)
    gatet_ref[...] = _gate_math(abt, alog_c_ref[...], dtb_c_ref[...], row < 2 * DN_HEADS)


def _inproj(x2d, seq, norm_w, w_in, conv_w, a_log, dt_bias, tm):
    t, d = x2d.shape
    hb = tm // CONV_HALO_ROWS
    last_halo = t // CONV_HALO_ROWS - 1
    n_qkv = 3 * DN_W
    n_ab = 4 * DN_HEADS
    bounds = [0, n_qkv, n_qkv + n_ab]
    for width in (DN_W, SG_W, SG_W, d, d):
        bounds.append(bounds[-1] + width)
    assert bounds[-1] == w_in.shape[1]
    piece = lambda k: w_in[:, bounds[k]:bounds[k + 1]].astype(BF16)
    w_qkv, w_ab = piece(0), piece(1)
    w_rest = [piece(k) for k in range(2, 7)]
    zeros8 = jnp.zeros((2 * DN_HEADS,), F32)
    alog16 = jnp.concatenate([zeros8, a_log.reshape(-1).astype(F32)])
    dtb16 = jnp.concatenate([zeros8, dt_bias.reshape(-1).astype(F32)])
    widths = [n_qkv, DN_W, SG_W, SG_W, d, d]
    out_shape = [jax.ShapeDtypeStruct((t, n), BF16) for n in widths]
    out_shape += [jax.ShapeDtypeStruct((t, n_ab), F32), jax.ShapeDtypeStruct((n_ab, t), F32)]
    out_specs = [pl.BlockSpec((tm, n), lambda i: (i, 0)) for n in widths]
    out_specs += [pl.BlockSpec((tm, n_ab), lambda i: (i, 0)), pl.BlockSpec((n_ab, tm), lambda i: (0, i))]
    consts = [norm_w.reshape(1, d).astype(F32), w_qkv, conv_w.astype(F32), *w_rest, w_ab, w_ab.T,
              alog16.reshape(1, n_ab), dtb16.reshape(1, n_ab), alog16.reshape(n_ab, 1), dtb16.reshape(n_ab, 1)]
    x_specs = [pl.BlockSpec((tm, d), lambda i: (i, 0)),
               pl.BlockSpec((CONV_HALO_ROWS, d), lambda i: (jnp.maximum(i * hb - 1, 0), 0)),
               pl.BlockSpec((CONV_HALO_ROWS, d), lambda i: (jnp.minimum((i + 1) * hb, last_halo), 0))]
    return pl.pallas_call(
        functools.partial(_inproj_kernel, tiles_per_seq=seq // tm),
        grid=(t // tm,),
        in_specs=x_specs + [_const_spec(c.shape) for c in consts],
        out_specs=out_specs,
        out_shape=out_shape,
        scratch_shapes=[pltpu.VMEM((tm + 2 * CONV_HALO_ROWS, n_qkv), F32)],
        compiler_params=_params("parallel"),
        name="inproj",
    )(x2d, x2d, x2d, *consts)


def _delta_kernel(qf_ref, gf_ref, gtf_ref, qb_ref, gb_ref, gtb_ref, of_ref, ob_ref, state_ref, *, cb):
    c = DN_CHUNK
    nc = cb // c
    qkv_refs = (qf_ref, qb_ref)

    @pl.when(pl.program_id(1) == 0)
    def _():
        state_ref[...] = jnp.zeros_like(state_ref)

    ri = lax.broadcasted_iota(jnp.int32, (cb, cb), 0)
    ci = lax.broadcasted_iota(jnp.int32, (cb, cb), 1)
    same = (ri & -c) == (ci & -c)
    ge = jnp.where(same & (ri >= ci), 1.0, 0.0)
    le = jnp.where(same & (ri <= ci), 1.0, 0.0)
    gates = (gf_ref[...], gb_ref[...])
    gc = (_dot_f32(ge, gates[0]), _dot_f32(le, gates[1]))
    gc_t = (_dot_f32(gtf_ref[...], le), _dot_f32(gtb_ref[...], ge))

    r64 = lax.broadcasted_iota(jnp.int32, (c, c), 0)
    c64 = lax.broadcasted_iota(jnp.int32, (c, c), 1)
    incl = (r64 >= c64, r64 <= c64)
    strict = (r64 > c64, r64 < c64)
    eye = jnp.where(r64 == c64, 1.0, 0.0)

    items = [(d, n, h) for d in range(2) for n in range(nc) for h in range(DN_HEADS)]
    q, k, v, beta, g_col, g_last, decay = {}, {}, {}, {}, {}, {}, {}
    for it in items:
        d, n, h = it
        rows = slice(n * c, (n + 1) * c)
        q[it] = qkv_refs[d][rows, h * DN_D:(h + 1) * DN_D].astype(F32)
        k[it] = qkv_refs[d][rows, DN_W + h * DN_D:DN_W + (h + 1) * DN_D].astype(F32)
        v[it] = qkv_refs[d][rows, 2 * DN_W + h * DN_D:2 * DN_W + (h + 1) * DN_D].astype(F32)
        beta[it] = gates[d][rows, d * DN_HEADS + h:d * DN_HEADS + h + 1]
        gcol = (2 + d) * DN_HEADS + h
        g_col[it] = gc[d][rows, gcol:gcol + 1]
        g_row = gc_t[d][gcol:gcol + 1, rows]
        g_last[it] = g_col[it][0:1, :] if d else g_col[it][c - 1:c, :]
        decay[it] = jnp.where(incl[d], jnp.exp(jnp.where(incl[d], g_col[it] - g_row, 0.0)), 0.0)
    kb = {it: k[it] * beta[it] for it in items}
    eg = {it: jnp.exp(g_col[it]) for it in items}
    kq = {it: _dot_nt(jnp.concatenate([kb[it], q[it]], axis=0), k[it]) for it in items}
    lm = {it: jnp.where(strict[it[0]], kq[it][:c] * decay[it], 0.0) for it in items}
    a_intra = {it: kq[it][c:] * decay[it] for it in items}
    x = {it: eye - lm[it] for it in items}
    p = {it: _dot(lm[it], lm[it]) for it in items}
    power = 2
    while 2 * power < c:
        x = {it: x[it] + _dot(x[it], p[it]) for it in items}
        p = {it: _dot(p[it], p[it]) for it in items}
        power *= 2
    tinv = {it: x[it] + _dot(x[it], p[it]) for it in items}
    uw = {it: _dot(tinv[it], jnp.concatenate([v[it] * beta[it], kb[it] * eg[it]], axis=1)) for it in items}
    wq = {it: jnp.concatenate([uw[it][:, DN_D:], q[it] * eg[it]], axis=0) for it in items}
    kg = {it: k[it] * jnp.exp(g_last[it] - g_col[it]) for it in items}

    state = {(d, h): state_ref[d, h] for d in range(2) for h in range(DN_HEADS)}
    o_refs = (of_ref, ob_ref)
    for s in range(nc):
        step = [(d, (nc - 1 - s) if d else s, h) for d in range(2) for h in range(DN_HEADS)]
        ws_qs = {it: _dot(wq[it], state[it[0], it[2]]) for it in step}
        v_new = {it: uw[it][:, :DN_D] - ws_qs[it][:c] for it in step}
        o = {it: ws_qs[it][c:] + _dot(a_intra[it], v_new[it]) for it in step}
        for it in step:
            d, n, h = it
            state[d, h] = state[d, h] * jnp.exp(g_last[it]) + _dot_tn(kg[it], v_new[it])
            o_refs[d][n * c:(n + 1) * c, h * DN_D:(h + 1) * DN_D] = o[it].astype(BF16)
    for (d, h), val in state.items():
        state_ref[d, h] = val


def _delta_scan(qkv, gates, gates_t, cb):
    bsz, seq, n_qkv = qkv.shape
    nb = seq // cb
    n_gate = gates.shape[1]

    def stream_specs(blk):
        return [
            pl.BlockSpec((None, cb, n_qkv), lambda b, i: (b, blk(i), 0)),
            pl.BlockSpec((cb, n_gate), lambda b, i: (b * nb + blk(i), 0)),
            pl.BlockSpec((n_gate, cb), lambda b, i: (0, b * nb + blk(i))),
        ]

    fwd = lambda i: i
    bwd = lambda i: nb - 1 - i
    stream = (qkv, gates, gates_t)
    return pl.pallas_call(
        functools.partial(_delta_kernel, cb=cb),
        grid=(bsz, nb),
        in_specs=stream_specs(fwd) + stream_specs(bwd),
        out_specs=[pl.BlockSpec((None, cb, DN_W), lambda b, i: (b, fwd(i), 0)),
                   pl.BlockSpec((None, cb, DN_W), lambda b, i: (b, bwd(i), 0))],
        out_shape=[jax.ShapeDtypeStruct((bsz, seq, DN_W), BF16)] * 2,
        scratch_shapes=[pltpu.VMEM((2, DN_HEADS, DN_D, DN_D), F32)],
        compiler_params=_params("parallel", "arbitrary"),
        name="delta",
    )(*stream, *stream)


def _mix_kernel(x_ref, of_ref, ob_ref, dng_ref, sgu_ref, sgv_ref, ga_ref, gb_ref,
                dnw_ref, wua_ref, lnw_ref, lnb_ref, ws_ref, bst_ref, wub_ref, wout_ref,
                y_ref, dn_s, sg_s):
    tm = x_ref.shape[0]
    for h in range(DN_HEADS):
        cols = slice(h * DN_D, (h + 1) * DN_D)
        o = of_ref[:, cols].astype(F32) + ob_ref[:, cols].astype(F32)
        dn_s[:, cols] = (_rms(o, dnw_ref[...]) * _silu(dng_ref[:, cols].astype(F32))).astype(BF16)
    y_a = _dot(dn_s[...], wua_ref[...])
    v = _gelu_tanh(sgv_ref[...].astype(F32))
    mu = jnp.mean(v, -1, keepdims=True)
    vc = v - mu
    v = vc * lax.rsqrt(jnp.mean(vc * vc, -1, keepdims=True) + EPS) * lnw_ref[...] + lnb_ref[...]
    v = v.astype(BF16)
    for n in range(tm // SG_CHUNK):
        rows = slice(n * SG_CHUNK, (n + 1) * SG_CHUNK)
        for g in range(SG_GROUPS):
            cols = slice(g * SG_CHUNK, (g + 1) * SG_CHUNK)
            mixed = _dot(ws_ref[g], v[rows, cols]) + bst_ref[:, g:g + 1]
            sg_s[rows, cols] = (_gelu_tanh(sgu_ref[rows, cols].astype(F32)) * mixed).astype(BF16)
    y_b = _dot(sg_s[...], wub_ref[...])
    merged = jax.nn.sigmoid(ga_ref[...].astype(F32)) * y_a + jax.nn.sigmoid(gb_ref[...].astype(F32)) * y_b
    y_ref[...] = x_ref[...] + _dot(merged, wout_ref[...])


def _mix(x2d, o_f, o_b, dng, sgu, sgv, ga, gb, dn_norm_w, w_up_a, ln_w, ln_b, sg_w, sg_b, w_up_b, w_out, tm):
    t, d = x2d.shape
    consts = [dn_norm_w.reshape(1, DN_D).astype(F32), w_up_a.astype(BF16),
              ln_w.reshape(1, SG_W).astype(F32), ln_b.reshape(1, SG_W).astype(F32),
              sg_w.astype(BF16), sg_b.T.astype(F32), w_up_b.astype(BF16), w_out.astype(BF16)]
    acts = [x2d, o_f, o_b, dng, sgu, sgv, ga, gb]
    return pl.pallas_call(
        _mix_kernel,
        grid=(t // tm,),
        in_specs=[pl.BlockSpec((tm, a.shape[1]), lambda i: (i, 0)) for a in acts]
                 + [_const_spec(c.shape) for c in consts],
        out_specs=pl.BlockSpec((tm, d), lambda i: (i, 0)),
        out_shape=jax.ShapeDtypeStruct((t, d), F32),
        scratch_shapes=[pltpu.VMEM((tm, DN_W), BF16), pltpu.VMEM((tm, SG_W), BF16)],
        compiler_params=_params("parallel"),
        name="mix",
    )(*acts, *consts)


def _norm_proj_kernel(x_ref, nw_ref, w_ref, o_ref):
    o_ref[...] = _dot(_rms(x_ref[...], nw_ref[...]), w_ref[...]).astype(o_ref.dtype)


def _norm_proj(x2d, norm_w, w, tm):
    t, d = x2d.shape
    n = w.shape[1]
    return pl.pallas_call(
        _norm_proj_kernel,
        grid=(t // tm,),
        in_specs=[pl.BlockSpec((tm, d), lambda i: (i, 0)), _const_spec((1, d)), _const_spec(w.shape)],
        out_specs=pl.BlockSpec((tm, n), lambda i: (i, 0)),
        out_shape=jax.ShapeDtypeStruct((t, n), BF16),
        compiler_params=_params("parallel"),
        name="mem_kv",
    )(x2d, norm_w.reshape(1, d).astype(F32), w.astype(BF16))


def _xattn_kernel(x_ref, kv_ref, nw_ref, wq_ref, wo_ref, y_ref, o_s):
    d = x_ref.shape[1]
    hd = d // XA_HEADS
    x = x_ref[...]
    q = _dot(_rms(x, nw_ref[...]), wq_ref[...])
    for h in range(XA_HEADS):
        cols = slice(h * hd, (h + 1) * hd)
        s = _dot_nt(q[:, cols], kv_ref[:, cols]) * (hd ** -0.5)
        p = jnp.exp(s - jnp.max(s, -1, keepdims=True))
        p = p / jnp.sum(p, -1, keepdims=True)
        o_s[:, cols] = _dot(p, kv_ref[:, d + h * hd:d + (h + 1) * hd]).astype(BF16)
    y_ref[...] = x + _dot(o_s[...], wo_ref[...])


def _xattn(x3d, kv, norm_w, w_q, w_o, tm):
    bsz, seq, d = x3d.shape
    n_mem = kv.shape[1]
    consts = [norm_w.reshape(1, d).astype(F32), w_q.astype(BF16), w_o.astype(BF16)]
    return pl.pallas_call(
        _xattn_kernel,
        grid=(bsz, seq // tm),
        in_specs=[pl.BlockSpec((None, tm, d), lambda b, i: (b, i, 0)),
                  pl.BlockSpec((None, n_mem, 2 * d), lambda b, i: (b, 0, 0))]
                 + [_const_spec(c.shape) for c in consts],
        out_specs=pl.BlockSpec((None, tm, d), lambda b, i: (b, i, 0)),
        out_shape=jax.ShapeDtypeStruct((bsz, seq, d), F32),
        scratch_shapes=[pltpu.VMEM((tm, d), BF16)],
        compiler_params=_params("parallel", "parallel"),
        name="xattn",
    )(x3d, kv, *consts)


def _ffn_kernel(x_ref, nw_ref, wg_ref, wu_ref, wd_ref, fw_ref, y_ref, *, ff_chunks):
    x = x_ref[...]
    h = _rms(x, nw_ref[...]).astype(BF16)
    acc = x
    for lo, hi in ff_chunks:
        a = _silu(_dot(h, wg_ref[:, lo:hi])) * _dot(h, wu_ref[:, lo:hi])
        acc = acc + _dot(a, wd_ref[lo:hi, :])
    y_ref[...] = _rms(acc, fw_ref[...])


def _ffn(x2d, norm_w, w_gate_up, w_down, final_w, tm):
    t, d = x2d.shape
    d_ff = w_down.shape[0]
    mxu_cols = 256
    assert d_ff % mxu_cols == 0
    half = (d_ff // mxu_cols + 1) // 2 * mxu_cols
    ff_chunks = ((0, half), (half, d_ff))
    consts = [norm_w.reshape(1, d).astype(F32), w_gate_up[:, :d_ff].astype(BF16),
              w_gate_up[:, d_ff:].astype(BF16), w_down.astype(BF16), final_w.reshape(1, d).astype(F32)]
    return pl.pallas_call(
        functools.partial(_ffn_kernel, ff_chunks=ff_chunks),
        grid=(t // tm,),
        in_specs=[pl.BlockSpec((tm, d), lambda i: (i, 0))] + [_const_spec(c.shape) for c in consts],
        out_specs=pl.BlockSpec((tm, d), lambda i: (i, 0)),
        out_shape=jax.ShapeDtypeStruct((t, d), F32),
        compiler_params=_params("parallel"),
        name="ffn",
    )(x2d, *consts)


def _largest_tile(n, cap):
    t = cap
    while n % t:
        t //= 2
    return t


def _trunk(x, mem, norm_mix_w, w_in, conv_w, dn_a_log, dn_dt_bias, dn_norm_w, w_up_a, sg_ln_w, sg_ln_b,
           sg_w, sg_b, w_up_b, w_out, norm_xa_w, norm_mem_w, xa_w_q, xa_w_kv, xa_w_o, norm_ffn_w,
           ffn_w_gate_up, ffn_w_down, final_norm_w):
    bsz, seq, d = x.shape
    t = bsz * seq
    assert seq % SG_CHUNK == 0 and seq % DN_CHUNK == 0
    tm = _largest_tile(seq, 512)
    cb = _largest_tile(seq, 128)
    x2d = x.reshape(t, d)
    qkv, dng, sgu, sgv, ga, gb, gates, gates_t = _inproj(x2d, seq, norm_mix_w, w_in, conv_w, dn_a_log,
                                                          dn_dt_bias, tm)
    o_f, o_b = _delta_scan(qkv.reshape(bsz, seq, -1), gates, gates_t, cb)
    x2d = _mix(x2d, o_f.reshape(t, DN_W), o_b.reshape(t, DN_W), dng, sgu, sgv, ga, gb, dn_norm_w, w_up_a,
               sg_ln_w, sg_ln_b, sg_w, sg_b, w_up_b, w_out, tm)
    n_mem = mem.shape[1]
    kv = _norm_proj(mem.reshape(bsz * n_mem, d), norm_mem_w, xa_w_kv, _largest_tile(bsz * n_mem, 256))
    x3d = _xattn(x2d.reshape(bsz, seq, d), kv.reshape(bsz, n_mem, 2 * d), norm_xa_w, xa_w_q, xa_w_o, tm)
    y = _ffn(x3d.reshape(t, d), norm_ffn_w, ffn_w_gate_up, ffn_w_down, final_norm_w, _largest_tile(seq, 256))
    return y.reshape(bsz, seq, d)


def kernel(x_prompt, x_sample, mem_prompt, mem_sample, norm_mix_w, w_in, conv_w, dn_a_log, dn_dt_bias, dn_norm_w, w_up_a, sg_ln_w, sg_ln_b, sg_w, sg_b, w_up_b, w_out, norm_xa_w, norm_mem_w, xa_w_q, xa_w_kv, xa_w_o, norm_ffn_w, ffn_w_gate_up, ffn_w_down, final_norm_w):
    depth = w_in.shape[0]
    assert depth == 1, "the FFN kernel fuses the final norm, which assumes a single layer"
    layer = (norm_mix_w[0], w_in[0], conv_w[0], dn_a_log[0], dn_dt_bias[0], dn_norm_w[0], w_up_a[0],
             sg_ln_w[0], sg_ln_b[0], sg_w[0], sg_b[0], w_up_b[0], w_out[0], norm_xa_w[0], norm_mem_w[0],
             xa_w_q[0], xa_w_kv[0], xa_w_o[0], norm_ffn_w[0], ffn_w_gate_up[0], ffn_w_down[0], final_norm_w)
    y_prompt = _trunk(x_prompt, mem_prompt, *layer)
    y_sample = _trunk(x_sample, mem_sample, *layer)
    return (y_prompt, y_sample)
```

```python
import functools

import jax
import jax.numpy as jnp
from jax import lax
from jax.experimental import pallas as pl
from jax.experimental.pallas import tpu as pltpu

F32 = jnp.float32
BF16 = jnp.bfloat16
EPS = 1e-6
LOG2_E = 1.4426950408889634

DN_HEADS = 4
DN_D = 128
DN_W = DN_HEADS * DN_D
DN_CHUNK = 64
CONV_K = 5
SG_GROUPS = 4
SG_CHUNK = 128
SG_W = SG_GROUPS * SG_CHUNK
XA_HEADS = 4

CONV_HALO_ROWS = 8
ROW_BLOCK = 64
DELTA_GROUP_STAGGER = 2
VMEM_LIMIT_BYTES = 56 * 1024 * 1024


def _dot(a, b):
    return jnp.dot(a.astype(BF16), b.astype(BF16), preferred_element_type=F32)


def _dot_nt(a, b):
    return lax.dot_general(a.astype(BF16), b.astype(BF16), (((1,), (1,)), ((), ())),
                           preferred_element_type=F32)


def _dot_tn(a, b):
    return lax.dot_general(a.astype(BF16), b.astype(BF16), (((0,), (0,)), ((), ())),
                           preferred_element_type=F32)


def _split_bf16(x):
    hi = x.astype(BF16)
    r = x - hi.astype(F32)
    mid = r.astype(BF16)
    lo = (r - mid.astype(F32)).astype(BF16)
    return hi, mid, lo


def _cumsum_rows(mask01, x):
    hi, mid, lo = _split_bf16(x)
    return _dot(mask01, hi) + (_dot(mask01, mid) + _dot(mask01, lo))


def _cumsum_lanes(x, mask01):
    m = x.shape[0]
    parts = _dot(jnp.concatenate(_split_bf16(x), axis=0), mask01)
    return parts[:m] + (parts[m:2 * m] + parts[2 * m:])


def _rms(x, w):
    return x * lax.rsqrt(jnp.mean(x * x, -1, keepdims=True) + EPS) * w


def _silu(x):
    return x * jax.nn.sigmoid(x)


def _gelu_tanh(x):
    c = 0.7978845608028654
    return x * (0.5 * (1.0 + jnp.tanh(c * (x + 0.044715 * (x * x * x)))))


def _params(*sem):
    return pltpu.CompilerParams(dimension_semantics=sem, vmem_limit_bytes=VMEM_LIMIT_BYTES)


def _const_spec(shape):
    nd = len(shape)
    return pl.BlockSpec(shape, lambda *_: (0,) * nd)


def _gate_math(ab, a_log, dt_bias, is_beta):
    z = ab + dt_bias
    softplus = jnp.maximum(z, 0.0) + jnp.log1p(jnp.exp(-jnp.abs(z)))
    return jnp.where(is_beta, jax.nn.sigmoid(ab), -jnp.exp(a_log) * softplus)


def _inproj_kernel(x_ref, xp_ref, xn_ref, nw_ref, wqkv_ref, cw_ref, wdng_ref, wsgu_ref, wsgv_ref, wga_ref, wgb_ref,
                   wabt_ref, alog_ref, dtb_ref,
                   qkv_ref, dng_ref, sgu_ref, sgv_ref, ga_ref, gb_ref, gatet_ref, raw_ref,
                   *, tiles_per_seq):
    tm = x_ref.shape[0]
    h0 = CONV_HALO_ROWS
    pad = CONV_K // 2
    mm_cols = 256
    pos = pl.program_id(0) % tiles_per_seq
    h = _rms(x_ref[...], nw_ref[...]).astype(BF16)
    raw_ref[0:h0, :] = _dot(_rms(xp_ref[...], nw_ref[...]), wqkv_ref[...]) * jnp.where(pos > 0, 1.0, 0.0)
    raw_ref[h0 + tm:2 * h0 + tm, :] = (_dot(_rms(xn_ref[...], nw_ref[...]), wqkv_ref[...])
                                       * jnp.where(pos < tiles_per_seq - 1, 1.0, 0.0))

    def qkv_chunk(j):
        raw_ref[h0:h0 + tm, j:j + mm_cols] = _dot(h, wqkv_ref[:, j:j + mm_cols])

    def conv_group(g):
        cols = slice(g * DN_D, (g + 1) * DN_D)
        for r in range(0, tm, ROW_BLOCK):
            lo = h0 - pad + r
            acc = raw_ref[lo:lo + ROW_BLOCK, cols] * cw_ref[0:1, cols]
            for j in range(1, CONV_K):
                acc = acc + raw_ref[lo + j:lo + j + ROW_BLOCK, cols] * cw_ref[j:j + 1, cols]
            y = _silu(acc)
            if g < 2 * DN_HEADS:
                y = y * lax.rsqrt(jnp.sum(y * y, -1, keepdims=True) + EPS)
            if g < DN_HEADS:
                y = y * (DN_D ** -0.5)
            qkv_ref[r:r + ROW_BLOCK, cols] = y.astype(BF16)

    rest = [(w_ref, o_ref, j)
            for w_ref, o_ref in ((wdng_ref, dng_ref), (wsgu_ref, sgu_ref), (wsgv_ref, sgv_ref),
                                 (wga_ref, ga_ref), (wgb_ref, gb_ref))
            for j in range(0, w_ref.shape[1], mm_cols)]

    def rest_chunk(w_ref, o_ref, j):
        o_ref[:, j:j + mm_cols] = _dot(h, w_ref[:, j:j + mm_cols]).astype(o_ref.dtype)

    groups_per_chunk = mm_cols // DN_D
    n_chunks = 3 * DN_W // mm_cols
    qkv_chunk(0)
    for ch in range(n_chunks):
        if ch + 1 < n_chunks:
            qkv_chunk((ch + 1) * mm_cols)
        for g in range(ch * groups_per_chunk, (ch + 1) * groups_per_chunk):
            conv_group(g)
            if rest:
                rest_chunk(*rest.pop(0))
    while rest:
        rest_chunk(*rest.pop(0))
    abt = _dot_nt(wabt_ref[...], h)
    row = lax.broadcasted_iota(jnp.int32, abt.shape, 0)
    gatet_ref[...] = _gate_math(abt, alog_ref[...], dtb_ref[...], row < 2 * DN_HEADS)


def _inproj(x2d, seq, norm_w, w_in, conv_w, a_log, dt_bias, tm):
    t, d = x2d.shape
    hb = tm // CONV_HALO_ROWS
    last_halo = t // CONV_HALO_ROWS - 1
    n_qkv = 3 * DN_W
    n_ab = 4 * DN_HEADS
    bounds = [0, n_qkv, n_qkv + n_ab]
    for width in (DN_W, SG_W, SG_W, d, d):
        bounds.append(bounds[-1] + width)
    assert bounds[-1] == w_in.shape[1]
    piece = lambda k: w_in[:, bounds[k]:bounds[k + 1]].astype(BF16)
    w_qkv, w_ab = piece(0), piece(1)
    w_rest = [piece(k) for k in range(2, 7)]
    zeros8 = jnp.zeros((2 * DN_HEADS,), F32)
    alog16 = jnp.concatenate([zeros8, a_log.reshape(-1).astype(F32)])
    dtb16 = jnp.concatenate([zeros8, dt_bias.reshape(-1).astype(F32)])
    widths = [n_qkv, DN_W, SG_W, SG_W, d, d]
    out_shape = [jax.ShapeDtypeStruct((t, n), BF16) for n in widths]
    out_shape += [jax.ShapeDtypeStruct((n_ab, t), F32)]
    out_specs = [pl.BlockSpec((tm, n), lambda i: (i, 0)) for n in widths]
    out_specs += [pl.BlockSpec((n_ab, tm), lambda i: (0, i))]
    consts = [norm_w.reshape(1, d).astype(F32), w_qkv, conv_w.astype(F32), *w_rest, w_ab.T,
              alog16.reshape(n_ab, 1), dtb16.reshape(n_ab, 1)]
    x_specs = [pl.BlockSpec((tm, d), lambda i: (i, 0)),
               pl.BlockSpec((CONV_HALO_ROWS, d), lambda i: (jnp.maximum(i * hb - 1, 0), 0)),
               pl.BlockSpec((CONV_HALO_ROWS, d), lambda i: (jnp.minimum((i + 1) * hb, last_halo), 0))]
    return pl.pallas_call(
        functools.partial(_inproj_kernel, tiles_per_seq=seq // tm),
        grid=(t // tm,),
        in_specs=x_specs + [_const_spec(c.shape) for c in consts],
        out_specs=out_specs,
        out_shape=out_shape,
        scratch_shapes=[pltpu.VMEM((tm + 2 * CONV_HALO_ROWS, n_qkv), F32)],
        compiler_params=_params("parallel"),
        name="inproj",
    )(x2d, x2d, x2d, *consts)


def _delta_kernel(qf_ref, gtf_ref, qb_ref, gtb_ref, of_ref, ob_ref, state_ref, *, cb):
    c = DN_CHUNK
    nc = cb // c
    qkv_refs = (qf_ref, qb_ref)
    n_gate = gtf_ref.shape[0]

    @pl.when(pl.program_id(1) == 0)
    def _():
        state_ref[...] = jnp.zeros_like(state_ref)

    ri = lax.broadcasted_iota(jnp.int32, (cb, cb), 0)
    ci = lax.broadcasted_iota(jnp.int32, (cb, cb), 1)
    same = (ri & -c) == (ci & -c)
    ge = jnp.where(same & (ri >= ci), 1.0, 0.0)
    le = jnp.where(same & (ri <= ci), 1.0, 0.0)
    gates_t = (gtf_ref[...], gtb_ref[...])
    gc_t = (_cumsum_lanes(gates_t[0], le) * LOG2_E, _cumsum_lanes(gates_t[1], ge) * LOG2_E)
    is_beta = lax.broadcasted_iota(jnp.int32, (n_gate, cb), 0) < 2 * DN_HEADS
    pad_rows = jnp.zeros((DN_D - n_gate, cb), F32)
    gc = tuple(jnp.concatenate([jnp.where(is_beta, gates_t[d], gc_t[d]), pad_rows], axis=0).T for d in range(2))
    gates = gc

    ri2 = lax.broadcasted_iota(jnp.int32, (c, 2 * c), 0)
    lane2 = lax.broadcasted_iota(jnp.int32, (c, 2 * c), 1)
    ci2 = lane2 & (c - 1)
    left = lane2 < c
    keep_a = (left & (ri2 <= ci2), left & (ri2 >= ci2))
    keep_l = (~left & (ri2 > ci2), ~left & (ri2 < ci2))
    sign = jnp.where(left, -1.0, 1.0)
    eye2 = jnp.where(ri2 == ci2, 1.0, 0.0)
    zeros2 = jnp.zeros((c, 2 * c), BF16)

    state = {(d, h): state_ref[d, h] for d in range(2) for h in range(DN_HEADS)}
    steps_done = [0]
    o_refs = (of_ref, ob_ref)

    def group(s):
        items = [(d, (nc - 1 - s) if d else s, h) for d in range(2) for h in range(DN_HEADS)]
        q, k, v, beta, g_col, g_last, decay = {}, {}, {}, {}, {}, {}, {}
        for it in items:
            d, n, h = it
            rows = slice(n * c, (n + 1) * c)
            q[it] = qkv_refs[d][rows, h * DN_D:(h + 1) * DN_D]
            k[it] = qkv_refs[d][rows, DN_W + h * DN_D:DN_W + (h + 1) * DN_D]
            v[it] = qkv_refs[d][rows, 2 * DN_W + h * DN_D:2 * DN_W + (h + 1) * DN_D]
        kqk = {it: _dot_nt(k[it], jnp.concatenate([q[it], k[it]], axis=0)) for it in items}
        yield
        for it in items:
            d, n, h = it
            rows = slice(n * c, (n + 1) * c)
            bcol = d * DN_HEADS + h
            gcol = (2 + d) * DN_HEADS + h
            beta[it] = jnp.broadcast_to(gates[d][rows, bcol:bcol + 1], (c, DN_D))
            g_col[it] = jnp.broadcast_to(gc[d][rows, gcol:gcol + 1], (c, DN_D))
            g_row = gc_t[d][gcol:gcol + 1, rows]
            g_last[it] = g_col[it][0:1, :] if d else g_col[it][c - 1:c, :]
            decay[it] = jnp.exp2((g_col[it] - jnp.concatenate([g_row, g_row], axis=1)) * sign)
        yield
        kqd = {it: kqk[it] * decay[it] for it in items}
        a_intra_t = {it: jnp.where(keep_a[it[0]], kqd[it], 0.0)[:, :c] for it in items}
        lm_r = {it: jnp.where(keep_l[it[0]], beta[it] * kqd[it], 0.0) for it in items}
        ll = {it: lm_r[it] + pltpu.roll(lm_r[it], c, 1) for it in items}
        sq = {it: _dot(ll[it], jnp.concatenate([lm_r[it].astype(BF16), zeros2], axis=0)) for it in items}
        yield
        z = {it: jnp.where(left, eye2 - ll[it], sq[it]) for it in items}
        eg = {it: jnp.exp2(g_col[it]) for it in items}
        uw_rhs = {it: jnp.concatenate([(v[it].astype(F32) * beta[it]).astype(BF16),
                                       (k[it].astype(F32) * (beta[it] * eg[it])).astype(BF16)], axis=1)
                  for it in items}
        uw_rhs = {it: jnp.concatenate([uw_rhs[it], jnp.zeros((c, 2 * DN_D), BF16)], axis=0) for it in items}
        power = 2
        while power < c:
            zb = {it: z[it].astype(BF16) for it in items}
            r = {it: _dot(zb[it], jnp.concatenate([zeros2, zb[it]], axis=0)) for it in items}
            yield
            keep = left if 2 * power < c else True
            z = {it: jnp.where(keep, z[it], 0.0) + r[it] for it in items}
            power *= 2
        uw = {it: _dot(z[it], uw_rhs[it]) for it in items}
        kga = {it: jnp.concatenate([(k[it].astype(F32) * jnp.exp2(g_last[it] - g_col[it])).astype(BF16),
                                    a_intra_t[it].astype(BF16)], axis=1) for it in items}
        yield
        assert steps_done[0] == s, "recurrence steps must be issued in order"
        wq = {it: jnp.concatenate([uw[it][:, DN_D:], q[it].astype(F32) * eg[it]], axis=0) for it in items}
        ws_qs = {it: _dot(wq[it], state[it[0], it[2]]) for it in items}
        yield
        v_new = {it: uw[it][:, :DN_D] - ws_qs[it][:c] for it in items}
        upd = {it: _dot_tn(kga[it], v_new[it]) for it in items}
        yield
        for it in items:
            d, n, h = it
            state[d, h] = state[d, h] * jnp.exp2(g_last[it]) + upd[it][:DN_D]
            o = ws_qs[it][c:] + upd[it][DN_D:]
            o_refs[d][n * c:(n + 1) * c, h * DN_D:(h + 1) * DN_D] = o.astype(BF16)
        steps_done[0] += 1

    waiting = [group(s) for s in range(nc)]
    live = []
    tick = 0
    while waiting or live:
        if waiting and tick % DELTA_GROUP_STAGGER == 0:
            live.append(waiting.pop(0))
        for g in list(live):
            try:
                next(g)
            except StopIteration:
                live.remove(g)
        tick += 1
    for (d, h), val in state.items():
        state_ref[d, h] = val


def _delta_scan(qkv, gates_t, cb):
    bsz, seq, n_qkv = qkv.shape
    nb = seq // cb
    n_gate = gates_t.shape[0]

    def stream_specs(blk):
        return [
            pl.BlockSpec((None, cb, n_qkv), lambda b, i: (b, blk(i), 0)),
            pl.BlockSpec((n_gate, cb), lambda b, i: (0, b * nb + blk(i))),
        ]

    fwd = lambda i: i
    bwd = lambda i: nb - 1 - i
    stream = (qkv, gates_t)
    return pl.pallas_call(
        functools.partial(_delta_kernel, cb=cb),
        grid=(bsz, nb),
        in_specs=stream_specs(fwd) + stream_specs(bwd),
        out_specs=[pl.BlockSpec((None, cb, DN_W), lambda b, i: (b, fwd(i), 0)),
                   pl.BlockSpec((None, cb, DN_W), lambda b, i: (b, bwd(i), 0))],
        out_shape=[jax.ShapeDtypeStruct((bsz, seq, DN_W), BF16)] * 2,
        scratch_shapes=[pltpu.VMEM((2, DN_HEADS, DN_D, DN_D), F32)],
        compiler_params=_params("parallel", "arbitrary"),
        name="delta",
    )(*stream, *stream)


def _mix_kernel(x_ref, of_ref, ob_ref, dng_ref, sgu_ref, sgv_ref, ga_ref, gb_ref,
                dnw_ref, wua_ref, lnw_ref, lnb_ref, ws_ref, bst_ref, wub_ref, wout_ref,
                y_ref, dn_s, sg_s):
    tm = x_ref.shape[0]
    for h in range(DN_HEADS):
        cols = slice(h * DN_D, (h + 1) * DN_D)
        o = of_ref[:, cols].astype(F32) + ob_ref[:, cols].astype(F32)
        dn_s[:, cols] = (_rms(o, dnw_ref[...]) * _silu(dng_ref[:, cols].astype(F32))).astype(BF16)
    y_a = _dot(dn_s[...], wua_ref[...])
    v = _gelu_tanh(sgv_ref[...].astype(F32))
    mu = jnp.mean(v, -1, keepdims=True)
    vc = v - mu
    v = vc * lax.rsqrt(jnp.mean(vc * vc, -1, keepdims=True) + EPS) * lnw_ref[...] + lnb_ref[...]
    v = v.astype(BF16)
    for n in range(tm // SG_CHUNK):
        rows = slice(n * SG_CHUNK, (n + 1) * SG_CHUNK)
        for g in range(SG_GROUPS):
            cols = slice(g * SG_CHUNK, (g + 1) * SG_CHUNK)
            mixed = _dot(ws_ref[g], v[rows, cols]) + bst_ref[:, g:g + 1]
            sg_s[rows, cols] = (_gelu_tanh(sgu_ref[rows, cols].astype(F32)) * mixed).astype(BF16)
    y_b = _dot(sg_s[...], wub_ref[...])
    merged = jax.nn.sigmoid(ga_ref[...].astype(F32)) * y_a + jax.nn.sigmoid(gb_ref[...].astype(F32)) * y_b
    y_ref[...] = x_ref[...] + _dot(merged, wout_ref[...])


def _mix(x2d, o_f, o_b, dng, sgu, sgv, ga, gb, dn_norm_w, w_up_a, ln_w, ln_b, sg_w, sg_b, w_up_b, w_out, tm):
    t, d = x2d.shape
    consts = [dn_norm_w.reshape(1, DN_D).astype(F32), w_up_a.astype(BF16),
              ln_w.reshape(1, SG_W).astype(F32), ln_b.reshape(1, SG_W).astype(F32),
              sg_w.astype(BF16), sg_b.T.astype(F32), w_up_b.astype(BF16), w_out.astype(BF16)]
    acts = [x2d, o_f, o_b, dng, sgu, sgv, ga, gb]
    return pl.pallas_call(
        _mix_kernel,
        grid=(t // tm,),
        in_specs=[pl.BlockSpec((tm, a.shape[1]), lambda i: (i, 0)) for a in acts]
                 + [_const_spec(c.shape) for c in consts],
        out_specs=pl.BlockSpec((tm, d), lambda i: (i, 0)),
        out_shape=jax.ShapeDtypeStruct((t, d), F32),
        scratch_shapes=[pltpu.VMEM((tm, DN_W), BF16), pltpu.VMEM((tm, SG_W), BF16)],
        compiler_params=_params("parallel"),
        name="mix",
    )(*acts, *consts)


def _norm_proj_kernel(x_ref, nw_ref, w_ref, o_ref):
    o_ref[...] = _dot(_rms(x_ref[...], nw_ref[...]), w_ref[...]).astype(o_ref.dtype)


def _norm_proj(x2d, norm_w, w, tm):
    t, d = x2d.shape
    n = w.shape[1]
    return pl.pallas_call(
        _norm_proj_kernel,
        grid=(t // tm,),
        in_specs=[pl.BlockSpec((tm, d), lambda i: (i, 0)), _const_spec((1, d)), _const_spec(w.shape)],
        out_specs=pl.BlockSpec((tm, n), lambda i: (i, 0)),
        out_shape=jax.ShapeDtypeStruct((t, n), BF16),
        compiler_params=_params("parallel"),
        name="mem_kv",
    )(x2d, norm_w.reshape(1, d).astype(F32), w.astype(BF16))


def _xattn_kernel(x_ref, kv_ref, nw_ref, wq_ref, wo_ref, y_ref, o_s):
    d = x_ref.shape[1]
    hd = d // XA_HEADS
    x = x_ref[...]
    q = _dot(_rms(x, nw_ref[...]), wq_ref[...])
    for h in range(XA_HEADS):
        cols = slice(h * hd, (h + 1) * hd)
        s = _dot_nt(q[:, cols], kv_ref[:, cols]) * (hd ** -0.5)
        p = jnp.exp(s - jnp.max(s, -1, keepdims=True))
        p = p / jnp.sum(p, -1, keepdims=True)
        o_s[:, cols] = _dot(p, kv_ref[:, d + h * hd:d + (h + 1) * hd]).astype(BF16)
    y_ref[...] = x + _dot(o_s[...], wo_ref[...])


def _xattn(x3d, kv, norm_w, w_q, w_o, tm):
    bsz, seq, d = x3d.shape
    n_mem = kv.shape[1]
    consts = [norm_w.reshape(1, d).astype(F32), w_q.astype(BF16), w_o.astype(BF16)]
    return pl.pallas_call(
        _xattn_kernel,
        grid=(bsz, seq // tm),
        in_specs=[pl.BlockSpec((None, tm, d), lambda b, i: (b, i, 0)),
                  pl.BlockSpec((None, n_mem, 2 * d), lambda b, i: (b, 0, 0))]
                 + [_const_spec(c.shape) for c in consts],
        out_specs=pl.BlockSpec((None, tm, d), lambda b, i: (b, i, 0)),
        out_shape=jax.ShapeDtypeStruct((bsz, seq, d), F32),
        scratch_shapes=[pltpu.VMEM((tm, d), BF16)],
        compiler_params=_params("parallel", "parallel"),
        name="xattn",
    )(x3d, kv, *consts)


def _ffn_kernel(x_ref, nw_ref, wg_ref, wu_ref, wd_ref, fw_ref, y_ref, *, ff_chunks):
    x = x_ref[...]
    h = _rms(x, nw_ref[...]).astype(BF16)
    acc = x
    for lo, hi in ff_chunks:
        a = _silu(_dot(h, wg_ref[:, lo:hi])) * _dot(h, wu_ref[:, lo:hi])
        acc = acc + _dot(a, wd_ref[lo:hi, :])
    y_ref[...] = _rms(acc, fw_ref[...])


def _ffn(x2d, norm_w, w_gate_up, w_down, final_w, tm):
    t, d = x2d.shape
    d_ff = w_down.shape[0]
    mxu_cols = 256
    assert d_ff % mxu_cols == 0
    half = (d_ff // mxu_cols + 1) // 2 * mxu_cols
    ff_chunks = ((0, half), (half, d_ff))
    consts = [norm_w.reshape(1, d).astype(F32), w_gate_up[:, :d_ff].astype(BF16),
              w_gate_up[:, d_ff:].astype(BF16), w_down.astype(BF16), final_w.reshape(1, d).astype(F32)]
    return pl.pallas_call(
        functools.partial(_ffn_kernel, ff_chunks=ff_chunks),
        grid=(t // tm,),
        in_specs=[pl.BlockSpec((tm, d), lambda i: (i, 0))] + [_const_spec(c.shape) for c in consts],
        out_specs=pl.BlockSpec((tm, d), lambda i: (i, 0)),
        out_shape=jax.ShapeDtypeStruct((t, d), F32),
        compiler_params=_params("parallel"),
        name="ffn",
    )(x2d, *consts)


def _largest_tile(n, cap):
    t = cap
    while n % t:
        t //= 2
    return t


def _trunk(x, mem, norm_mix_w, w_in, conv_w, dn_a_log, dn_dt_bias, dn_norm_w, w_up_a, sg_ln_w, sg_ln_b,
           sg_w, sg_b, w_up_b, w_out, norm_xa_w, norm_mem_w, xa_w_q, xa_w_kv, xa_w_o, norm_ffn_w,
           ffn_w_gate_up, ffn_w_down, final_norm_w):
    bsz, seq, d = x.shape
    t = bsz * seq
    assert seq % SG_CHUNK == 0 and seq % DN_CHUNK == 0
    tm = _largest_tile(seq, 512)
    cb = _largest_tile(seq, 512)
    x2d = x.reshape(t, d)
    qkv, dng, sgu, sgv, ga, gb, gates_t = _inproj(x2d, seq, norm_mix_w, w_in, conv_w, dn_a_log, dn_dt_bias, tm)
    o_f, o_b = _delta_scan(qkv.reshape(bsz, seq, -1), gates_t, cb)
    x2d = _mix(x2d, o_f.reshape(t, DN_W), o_b.reshape(t, DN_W), dng, sgu, sgv, ga, gb, dn_norm_w, w_up_a,
               sg_ln_w, sg_ln_b, sg_w, sg_b, w_up_b, w_out, tm)
    n_mem = mem.shape[1]
    kv = _norm_proj(mem.reshape(bsz * n_mem, d), norm_mem_w, xa_w_kv, _largest_tile(bsz * n_mem, 256))
    x3d = _xattn(x2d.reshape(bsz, seq, d), kv.reshape(bsz, n_mem, 2 * d), norm_xa_w, xa_w_q, xa_w_o, tm)
    y = _ffn(x3d.reshape(t, d), norm_ffn_w, ffn_w_gate_up, ffn_w_down, final_norm_w, tm)
    return y.reshape(bsz, seq, d)


def kernel(x_prompt, x_sample, mem_prompt, mem_sample, norm_mix_w, w_in, conv_w, dn_a_log, dn_dt_bias, dn_norm_w, w_up_a, sg_ln_w, sg_ln_b, sg_w, sg_b, w_up_b, w_out, norm_xa_w, norm_mem_w, xa_w_q, xa_w_kv, xa_w_o, norm_ffn_w, ffn_w_gate_up, ffn_w_down, final_norm_w):
    depth = w_in.shape[0]
    assert depth == 1, "the FFN kernel fuses the final norm, which assumes a single layer"
    layer = (norm_mix_w[0], w_in[0], conv_w[0], dn_a_log[0], dn_dt_bias[0], dn_norm_w[0], w_up_a[0],
             sg_ln_w[0], sg_ln_b[0], sg_w[0], sg_b[0], w_up_b[0], w_out[0], norm_xa_w[0], norm_mem_w[0],
             xa_w_q[0], xa_w_kv[0], xa_w_o[0], norm_ffn_w[0], ffn_w_gate_up[0], ffn_w_down[0], final_norm_w)
    y_prompt = _trunk(x_prompt, mem_prompt, *layer)
    y_sample = _trunk(x_sample, mem_sample, *layer)
    return (y_prompt, y_sample)
```

```python
import functools

import jax
import jax.numpy as jnp
from jax import lax
from jax.experimental import pallas as pl
from jax.experimental.pallas import tpu as pltpu

F32 = jnp.float32
BF16 = jnp.bfloat16
EPS = 1e-6
LOG2_E = 1.4426950408889634

DN_HEADS = 4
DN_D = 128
DN_W = DN_HEADS * DN_D
DN_CHUNK = 64
CONV_K = 5
SG_GROUPS = 4
SG_CHUNK = 128
SG_W = SG_GROUPS * SG_CHUNK
XA_HEADS = 4

CONV_HALO_ROWS = 8
ROW_BLOCK = 64
DELTA_GROUP_STAGGER = 2
VMEM_LIMIT_BYTES = 56 * 1024 * 1024


def _dot(a, b):
    return jnp.dot(a.astype(BF16), b.astype(BF16), preferred_element_type=F32)


def _dot_nt(a, b):
    return lax.dot_general(a.astype(BF16), b.astype(BF16), (((1,), (1,)), ((), ())),
                           preferred_element_type=F32)


def _dot_tn(a, b):
    return lax.dot_general(a.astype(BF16), b.astype(BF16), (((0,), (0,)), ((), ())),
                           preferred_element_type=F32)


def _split_bf16(x):
    hi = x.astype(BF16)
    r = x - hi.astype(F32)
    mid = r.astype(BF16)
    lo = (r - mid.astype(F32)).astype(BF16)
    return hi, mid, lo


def _cumsum_rows(mask01, x):
    hi, mid, lo = _split_bf16(x)
    return _dot(mask01, hi) + (_dot(mask01, mid) + _dot(mask01, lo))


def _cumsum_lanes(x, mask01):
    m = x.shape[0]
    parts = _dot(jnp.concatenate(_split_bf16(x), axis=0), mask01)
    return parts[:m] + (parts[m:2 * m] + parts[2 * m:])


def _zero_after(x):
    bits = pltpu.bitcast(x[0:8, 0:128], jnp.uint32)
    sixteen = jnp.uint32(16)
    return lax.shift_right_logical(lax.shift_right_logical(bits, sixteen), sixteen).astype(F32)


def _rms(x, w):
    return x * lax.rsqrt(jnp.mean(x * x, -1, keepdims=True) + EPS) * w


def _silu(x):
    h = 0.5 * x
    return h + h * jnp.tanh(h)


def _gelu_tanh(x):
    c = 0.7978845608028654
    h = 0.5 * x
    return h + h * jnp.tanh(x * (c + (c * 0.044715) * (x * x)))


def _sigmoid_mix(ga, ya, gb, yb):
    return 0.5 * ((ya + yb) + (jnp.tanh(0.5 * ga) * ya + jnp.tanh(0.5 * gb) * yb))


def _params(*sem):
    return pltpu.CompilerParams(dimension_semantics=sem, vmem_limit_bytes=VMEM_LIMIT_BYTES)


def _const_spec(shape):
    nd = len(shape)
    return pl.BlockSpec(shape, lambda *_: (0,) * nd)


def _gate_math(ab, a_log, dt_bias, is_beta):
    z = ab + dt_bias
    softplus = jnp.maximum(z, 0.0) + jnp.log1p(jnp.exp(-jnp.abs(z)))
    return jnp.where(is_beta, jax.nn.sigmoid(ab), -jnp.exp(a_log) * softplus)


def _inproj_kernel(x_ref, xp_ref, xn_ref, nw_ref, wqkv_ref, cw_ref, wdng_ref, wsgu_ref, wsgv_ref, wga_ref, wgb_ref,
                   wabt_ref, alog_ref, dtb_ref,
                   qkv_ref, dng_ref, sgu_ref, sgv_ref, ga_ref, gb_ref, gatet_ref, raw_ref,
                   *, tiles_per_seq):
    tm = x_ref.shape[0]
    h0 = CONV_HALO_ROWS
    pad = CONV_K // 2
    mm_cols = 256
    pos = pl.program_id(0) % tiles_per_seq
    half = tm // 2
    h = _rms(x_ref[...], nw_ref[...]).astype(BF16)
    hs = {r0: h[r0:r0 + half] for r0 in (0, half)}
    raw_ref[0:h0, :] = _dot(_rms(xp_ref[...], nw_ref[...]), wqkv_ref[...]) * jnp.where(pos > 0, 1.0, 0.0)
    raw_ref[h0 + tm:2 * h0 + tm, :] = (_dot(_rms(xn_ref[...], nw_ref[...]), wqkv_ref[...])
                                       * jnp.where(pos < tiles_per_seq - 1, 1.0, 0.0))

    def qkv_chunk(j):
        raw_ref[h0:h0 + tm, j:j + mm_cols] = _dot(h, wqkv_ref[:, j:j + mm_cols])

    def conv_unit(g, r, after):
        cols = slice(g * DN_D, (g + 1) * DN_D)
        lo = h0 - pad + r
        acc = raw_ref[lo:lo + ROW_BLOCK, cols] * (cw_ref[0:1, cols] + after[0:1, :])
        for j in range(1, CONV_K):
            acc = acc + raw_ref[lo + j:lo + j + ROW_BLOCK, cols] * cw_ref[j:j + 1, cols]
        y = _silu(acc)
        if g < 2 * DN_HEADS:
            y = y * lax.rsqrt(jnp.sum(y * y, -1, keepdims=True) + EPS)
        if g < DN_HEADS:
            y = y * (DN_D ** -0.5)
        qkv_ref[r:r + ROW_BLOCK, cols] = y.astype(BF16)

    rest = [(w_ref, o_ref, j, r0)
            for w_ref, o_ref in ((wdng_ref, dng_ref), (wsgu_ref, sgu_ref), (wsgv_ref, sgv_ref),
                                 (wga_ref, ga_ref), (wgb_ref, gb_ref))
            for j in range(0, w_ref.shape[1], mm_cols) for r0 in (0, half)]
    n_rest = len(rest)

    def rest_unit(w_ref, o_ref, j, r0):
        res = _dot(hs[r0], w_ref[:, j:j + mm_cols])
        o_ref[r0:r0 + half, j:j + mm_cols] = res.astype(o_ref.dtype)
        return _zero_after(res)

    groups_per_chunk = mm_cols // DN_D
    n_chunks = 3 * DN_W // mm_cols
    n_conv = 3 * DN_HEADS * (tm // ROW_BLOCK)
    done = 0
    after = jnp.zeros((8, DN_D), F32)
    qkv_chunk(0)
    for ch in range(n_chunks):
        if ch + 1 < n_chunks:
            qkv_chunk((ch + 1) * mm_cols)
        for g in range(ch * groups_per_chunk, (ch + 1) * groups_per_chunk):
            for r in range(0, tm, ROW_BLOCK):
                conv_unit(g, r, after)
                done += 1
                while rest and (n_rest - len(rest)) * n_conv < done * n_rest:
                    after = rest_unit(*rest.pop(0))
    while rest:
        rest_unit(*rest.pop(0))
    abt = _dot_nt(wabt_ref[...], h)
    row = lax.broadcasted_iota(jnp.int32, abt.shape, 0)
    gatet_ref[...] = _gate_math(abt, alog_ref[...], dtb_ref[...], row < 2 * DN_HEADS)


def _inproj(x2d, seq, norm_w, w_in, conv_w, a_log, dt_bias, tm):
    t, d = x2d.shape
    hb = tm // CONV_HALO_ROWS
    last_halo = t // CONV_HALO_ROWS - 1
    n_qkv = 3 * DN_W
    n_ab = 4 * DN_HEADS
    bounds = [0, n_qkv, n_qkv + n_ab]
    for width in (DN_W, SG_W, SG_W, d, d):
        bounds.append(bounds[-1] + width)
    assert bounds[-1] == w_in.shape[1]
    piece = lambda k: w_in[:, bounds[k]:bounds[k + 1]].astype(BF16)
    w_qkv, w_ab = piece(0), piece(1)
    w_rest = [piece(k) for k in range(2, 7)]
    zeros8 = jnp.zeros((2 * DN_HEADS,), F32)
    alog16 = jnp.concatenate([zeros8, a_log.reshape(-1).astype(F32)])
    dtb16 = jnp.concatenate([zeros8, dt_bias.reshape(-1).astype(F32)])
    widths = [n_qkv, DN_W, SG_W, SG_W, d, d]
    out_shape = [jax.ShapeDtypeStruct((t, n), BF16) for n in widths]
    out_shape += [jax.ShapeDtypeStruct((n_ab, t), F32)]
    out_specs = [pl.BlockSpec((tm, n), lambda i: (i, 0)) for n in widths]
    out_specs += [pl.BlockSpec((n_ab, tm), lambda i: (0, i))]
    consts = [norm_w.reshape(1, d).astype(F32), w_qkv, conv_w.astype(F32), *w_rest, w_ab.T,
              alog16.reshape(n_ab, 1), dtb16.reshape(n_ab, 1)]
    x_specs = [pl.BlockSpec((tm, d), lambda i: (i, 0)),
               pl.BlockSpec((CONV_HALO_ROWS, d), lambda i: (jnp.maximum(i * hb - 1, 0), 0)),
               pl.BlockSpec((CONV_HALO_ROWS, d), lambda i: (jnp.minimum((i + 1) * hb, last_halo), 0))]
    return pl.pallas_call(
        functools.partial(_inproj_kernel, tiles_per_seq=seq // tm),
        grid=(t // tm,),
        in_specs=x_specs + [_const_spec(c.shape) for c in consts],
        out_specs=out_specs,
        out_shape=out_shape,
        scratch_shapes=[pltpu.VMEM((tm + 2 * CONV_HALO_ROWS, n_qkv), F32)],
        compiler_params=_params("parallel"),
        name="inproj",
    )(x2d, x2d, x2d, *consts)


def _delta_kernel(qf_ref, gtf_ref, qb_ref, gtb_ref, of_ref, ob_ref, state_ref, *, cb):
    c = DN_CHUNK
    nc = cb // c
    qkv_refs = (qf_ref, qb_ref)
    n_gate = gtf_ref.shape[0]

    @pl.when(pl.program_id(1) == 0)
    def _():
        state_ref[...] = jnp.zeros_like(state_ref)

    ri = lax.broadcasted_iota(jnp.int32, (cb, cb), 0)
    ci = lax.broadcasted_iota(jnp.int32, (cb, cb), 1)
    same = (ri & -c) == (ci & -c)
    ge = jnp.where(same & (ri >= ci), 1.0, 0.0)
    le = jnp.where(same & (ri <= ci), 1.0, 0.0)
    gates_t = (gtf_ref[...], gtb_ref[...])
    gc_t = (_cumsum_lanes(gates_t[0], le) * LOG2_E, _cumsum_lanes(gates_t[1], ge) * LOG2_E)
    is_beta = lax.broadcasted_iota(jnp.int32, (n_gate, cb), 0) < 2 * DN_HEADS
    pad_rows = jnp.zeros((DN_D - n_gate, cb), F32)
    gc = tuple(jnp.concatenate([jnp.where(is_beta, gates_t[d], gc_t[d]), pad_rows], axis=0).T for d in range(2))
    gates = gc

    ri2 = lax.broadcasted_iota(jnp.int32, (c, 2 * c), 0)
    lane2 = lax.broadcasted_iota(jnp.int32, (c, 2 * c), 1)
    ci2 = lane2 & (c - 1)
    left = lane2 < c
    keep_a = (left & (ri2 <= ci2), left & (ri2 >= ci2))
    keep_l = (~left & (ri2 > ci2), ~left & (ri2 < ci2))
    sign = jnp.where(left, -1.0, 1.0)
    eye2 = jnp.where(ri2 == ci2, 1.0, 0.0)
    zeros2 = jnp.zeros((c, 2 * c), BF16)

    state = {(d, h): state_ref[d, h] for d in range(2) for h in range(DN_HEADS)}
    steps_done = [0]
    o_refs = (of_ref, ob_ref)

    def group(s):
        items = [(d, (nc - 1 - s) if d else s, h) for d in range(2) for h in range(DN_HEADS)]
        q, k, v, beta, g_col, g_last, decay = {}, {}, {}, {}, {}, {}, {}
        for it in items:
            d, n, h = it
            rows = slice(n * c, (n + 1) * c)
            q[it] = qkv_refs[d][rows, h * DN_D:(h + 1) * DN_D]
            k[it] = qkv_refs[d][rows, DN_W + h * DN_D:DN_W + (h + 1) * DN_D]
            v[it] = qkv_refs[d][rows, 2 * DN_W + h * DN_D:2 * DN_W + (h + 1) * DN_D]
        kqk = {it: _dot_nt(k[it], jnp.concatenate([q[it], k[it]], axis=0)) for it in items}
        yield
        for it in items:
            d, n, h = it
            rows = slice(n * c, (n + 1) * c)
            bcol = d * DN_HEADS + h
            gcol = (2 + d) * DN_HEADS + h
            beta[it] = jnp.broadcast_to(gates[d][rows, bcol:bcol + 1], (c, DN_D))
            g_col[it] = jnp.broadcast_to(gc[d][rows, gcol:gcol + 1], (c, DN_D))
            g_row = gc_t[d][gcol:gcol + 1, rows]
            g_last[it] = g_col[it][0:1, :] if d else g_col[it][c - 1:c, :]
            decay[it] = jnp.exp2((g_col[it] - jnp.concatenate([g_row, g_row], axis=1)) * sign)
        yield
        kqd = {it: kqk[it] * decay[it] for it in items}
        a_intra_t = {it: jnp.where(keep_a[it[0]], kqd[it], 0.0)[:, :c] for it in items}
        lm_r = {it: jnp.where(keep_l[it[0]], beta[it] * kqd[it], 0.0) for it in items}
        ll = {it: lm_r[it] + pltpu.roll(lm_r[it], c, 1) for it in items}
        sq = {it: _dot(ll[it], jnp.concatenate([lm_r[it].astype(BF16), zeros2], axis=0)) for it in items}
        yield
        z = {it: jnp.where(left, eye2 - ll[it], sq[it]) for it in items}
        eg = {it: jnp.exp2(g_col[it]) for it in items}
        uw_rhs = {it: jnp.concatenate([(v[it].astype(F32) * beta[it]).astype(BF16),
                                       (k[it].astype(F32) * (beta[it] * eg[it])).astype(BF16)], axis=1)
                  for it in items}
        uw_rhs = {it: jnp.concatenate([uw_rhs[it], jnp.zeros((c, 2 * DN_D), BF16)], axis=0) for it in items}
        power = 2
        while power < c:
            zb = {it: z[it].astype(BF16) for it in items}
            r = {it: _dot(zb[it], jnp.concatenate([zeros2, zb[it]], axis=0)) for it in items}
            yield
            keep = left if 2 * power < c else True
            z = {it: jnp.where(keep, z[it], 0.0) + r[it] for it in items}
            power *= 2
        uw = {it: _dot(z[it], uw_rhs[it]) for it in items}
        kga = {it: jnp.concatenate([(k[it].astype(F32) * jnp.exp2(g_last[it] - g_col[it])).astype(BF16),
                                    a_intra_t[it].astype(BF16)], axis=1) for it in items}
        yield
        assert steps_done[0] == s, "recurrence steps must be issued in order"
        wq = {it: jnp.concatenate([uw[it][:, DN_D:], q[it].astype(F32) * eg[it]], axis=0) for it in items}
        ws_qs = {it: _dot(wq[it], state[it[0], it[2]]) for it in items}
        yield
        v_new = {it: uw[it][:, :DN_D] - ws_qs[it][:c] for it in items}
        upd = {it: _dot_tn(kga[it], v_new[it]) for it in items}
        yield
        for it in items:
            d, n, h = it
            state[d, h] = state[d, h] * jnp.exp2(g_last[it]) + upd[it][:DN_D]
            o = ws_qs[it][c:] + upd[it][DN_D:]
            o_refs[d][n * c:(n + 1) * c, h * DN_D:(h + 1) * DN_D] = o.astype(BF16)
        steps_done[0] += 1

    waiting = [group(s) for s in range(nc)]
    live = []
    tick = 0
    while waiting or live:
        if waiting and tick % DELTA_GROUP_STAGGER == 0:
            live.append(waiting.pop(0))
        for g in list(live):
            try:
                next(g)
            except StopIteration:
                live.remove(g)
        tick += 1
    for (d, h), val in state.items():
        state_ref[d, h] = val


def _delta_scan(qkv, gates_t, cb):
    bsz, seq, n_qkv = qkv.shape
    nb = seq // cb
    n_gate = gates_t.shape[0]

    def stream_specs(blk):
        return [
            pl.BlockSpec((None, cb, n_qkv), lambda b, i: (b, blk(i), 0)),
            pl.BlockSpec((n_gate, cb), lambda b, i: (0, b * nb + blk(i))),
        ]

    fwd = lambda i: i
    bwd = lambda i: nb - 1 - i
    stream = (qkv, gates_t)
    return pl.pallas_call(
        functools.partial(_delta_kernel, cb=cb),
        grid=(bsz, nb),
        in_specs=stream_specs(fwd) + stream_specs(bwd),
        out_specs=[pl.BlockSpec((None, cb, DN_W), lambda b, i: (b, fwd(i), 0)),
                   pl.BlockSpec((None, cb, DN_W), lambda b, i: (b, bwd(i), 0))],
        out_shape=[jax.ShapeDtypeStruct((bsz, seq, DN_W), BF16)] * 2,
        scratch_shapes=[pltpu.VMEM((2, DN_HEADS, DN_D, DN_D), F32)],
        compiler_params=_params("parallel", "arbitrary"),
        name="delta",
    )(*stream, *stream)


def _mix_kernel(x_ref, of_ref, ob_ref, dng_ref, sgu_ref, sgv_ref, ga_ref, gb_ref,
                dnw_ref, wua_ref, lnw_ref, lnb_ref, ws_ref, bst_ref, wub_ref, wout_ref,
                y_ref, dn_s, sg_s):
    tm = x_ref.shape[0]
    for h in range(DN_HEADS):
        cols = slice(h * DN_D, (h + 1) * DN_D)
        o = of_ref[:, cols].astype(F32) + ob_ref[:, cols].astype(F32)
        dn_s[:, cols] = (_rms(o, dnw_ref[...]) * _silu(dng_ref[:, cols].astype(F32))).astype(BF16)
    y_a = _dot(dn_s[...], wua_ref[...])
    v = _gelu_tanh(sgv_ref[...].astype(F32))
    mu = jnp.mean(v, -1, keepdims=True)
    vc = v - mu
    v = vc * lax.rsqrt(jnp.mean(vc * vc, -1, keepdims=True) + EPS) * lnw_ref[...] + lnb_ref[...]
    v = v.astype(BF16)
    for n in range(tm // SG_CHUNK):
        rows = slice(n * SG_CHUNK, (n + 1) * SG_CHUNK)
        for g in range(SG_GROUPS):
            cols = slice(g * SG_CHUNK, (g + 1) * SG_CHUNK)
            mixed = _dot(ws_ref[g], v[rows, cols]) + bst_ref[:, g:g + 1]
            sg_s[rows, cols] = (_gelu_tanh(sgu_ref[rows, cols].astype(F32)) * mixed).astype(BF16)
    y_b = _dot(sg_s[...], wub_ref[...])
    merged = _sigmoid_mix(ga_ref[...].astype(F32), y_a, gb_ref[...].astype(F32), y_b)
    y_ref[...] = x_ref[...] + _dot(merged, wout_ref[...])


def _mix(x2d, o_f, o_b, dng, sgu, sgv, ga, gb, dn_norm_w, w_up_a, ln_w, ln_b, sg_w, sg_b, w_up_b, w_out, tm):
    t, d = x2d.shape
    consts = [dn_norm_w.reshape(1, DN_D).astype(F32), w_up_a.astype(BF16),
              ln_w.reshape(1, SG_W).astype(F32), ln_b.reshape(1, SG_W).astype(F32),
              sg_w.astype(BF16), sg_b.T.astype(F32), w_up_b.astype(BF16), w_out.astype(BF16)]
    acts = [x2d, o_f, o_b, dng, sgu, sgv, ga, gb]
    return pl.pallas_call(
        _mix_kernel,
        grid=(t // tm,),
        in_specs=[pl.BlockSpec((tm, a.shape[1]), lambda i: (i, 0)) for a in acts]
                 + [_const_spec(c.shape) for c in consts],
        out_specs=pl.BlockSpec((tm, d), lambda i: (i, 0)),
        out_shape=jax.ShapeDtypeStruct((t, d), F32),
        scratch_shapes=[pltpu.VMEM((tm, DN_W), BF16), pltpu.VMEM((tm, SG_W), BF16)],
        compiler_params=_params("parallel"),
        name="mix",
    )(*acts, *consts)


def _norm_proj_kernel(x_ref, nw_ref, w_ref, o_ref):
    o_ref[...] = _dot(_rms(x_ref[...], nw_ref[...]), w_ref[...]).astype(o_ref.dtype)


def _norm_proj(x2d, norm_w, w, tm):
    t, d = x2d.shape
    n = w.shape[1]
    return pl.pallas_call(
        _norm_proj_kernel,
        grid=(t // tm,),
        in_specs=[pl.BlockSpec((tm, d), lambda i: (i, 0)), _const_spec((1, d)), _const_spec(w.shape)],
        out_specs=pl.BlockSpec((tm, n), lambda i: (i, 0)),
        out_shape=jax.ShapeDtypeStruct((t, n), BF16),
        compiler_params=_params("parallel"),
        name="mem_kv",
    )(x2d, norm_w.reshape(1, d).astype(F32), w.astype(BF16))


def _xattn_kernel(x_ref, kv_ref, nw_ref, wq_ref, wo_ref, y_ref, o_s):
    d = x_ref.shape[1]
    hd = d // XA_HEADS
    x = x_ref[...]
    q = _dot(_rms(x, nw_ref[...]), wq_ref[...])
    for h in range(XA_HEADS):
        cols = slice(h * hd, (h + 1) * hd)
        s = _dot_nt(q[:, cols], kv_ref[:, cols]) * (hd ** -0.5)
        p = jnp.exp(s - jnp.max(s, -1, keepdims=True))
        p = p / jnp.sum(p, -1, keepdims=True)
        o_s[:, cols] = _dot(p, kv_ref[:, d + h * hd:d + (h + 1) * hd]).astype(BF16)
    y_ref[...] = x + _dot(o_s[...], wo_ref[...])


def _xattn(x3d, kv, norm_w, w_q, w_o, tm):
    bsz, seq, d = x3d.shape
    n_mem = kv.shape[1]
    consts = [norm_w.reshape(1, d).astype(F32), w_q.astype(BF16), w_o.astype(BF16)]
    return pl.pallas_call(
        _xattn_kernel,
        grid=(bsz, seq // tm),
        in_specs=[pl.BlockSpec((None, tm, d), lambda b, i: (b, i, 0)),
                  pl.BlockSpec((None, n_mem, 2 * d), lambda b, i: (b, 0, 0))]
                 + [_const_spec(c.shape) for c in consts],
        out_specs=pl.BlockSpec((None, tm, d), lambda b, i: (b, i, 0)),
        out_shape=jax.ShapeDtypeStruct((bsz, seq, d), F32),
        scratch_shapes=[pltpu.VMEM((tm, d), BF16)],
        compiler_params=_params("parallel", "parallel"),
        name="xattn",
    )(x3d, kv, *consts)


def _ffn_kernel(x_ref, nw_ref, wg_ref, wu_ref, wd_ref, fw_ref, y_ref, *, ff_chunks):
    x = x_ref[...]
    h = _rms(x, nw_ref[...]).astype(BF16)
    acc = x
    for lo, hi in ff_chunks:
        a = _silu(_dot(h, wg_ref[:, lo:hi])) * _dot(h, wu_ref[:, lo:hi])
        acc = acc + _dot(a, wd_ref[lo:hi, :])
    y_ref[...] = _rms(acc, fw_ref[...])


def _ffn(x2d, norm_w, w_gate_up, w_down, final_w, tm):
    t, d = x2d.shape
    d_ff = w_down.shape[0]
    mxu_cols = 256
    assert d_ff % mxu_cols == 0
    half = (d_ff // mxu_cols + 1) // 2 * mxu_cols
    ff_chunks = ((0, half), (half, d_ff))
    consts = [norm_w.reshape(1, d).astype(F32), w_gate_up[:, :d_ff].astype(BF16),
              w_gate_up[:, d_ff:].astype(BF16), w_down.astype(BF16), final_w.reshape(1, d).astype(F32)]
    return pl.pallas_call(
        functools.partial(_ffn_kernel, ff_chunks=ff_chunks),
        grid=(t // tm,),
        in_specs=[pl.BlockSpec((tm, d), lambda i: (i, 0))] + [_const_spec(c.shape) for c in consts],
        out_specs=pl.BlockSpec((tm, d), lambda i: (i, 0)),
        out_shape=jax.ShapeDtypeStruct((t, d), F32),
        compiler_params=_params("parallel"),
        name="ffn",
    )(x2d, *consts)


def _largest_tile(n, cap):
    t = cap
    while n % t:
        t //= 2
    return t


def _trunk(x, mem, norm_mix_w, w_in, conv_w, dn_a_log, dn_dt_bias, dn_norm_w, w_up_a, sg_ln_w, sg_ln_b,
           sg_w, sg_b, w_up_b, w_out, norm_xa_w, norm_mem_w, xa_w_q, xa_w_kv, xa_w_o, norm_ffn_w,
           ffn_w_gate_up, ffn_w_down, final_norm_w):
    bsz, seq, d = x.shape
    t = bsz * seq
    assert seq % SG_CHUNK == 0 and seq % DN_CHUNK == 0
    tm = _largest_tile(seq, 512)
    cb = _largest_tile(seq, 512)
    x2d = x.reshape(t, d)
    qkv, dng, sgu, sgv, ga, gb, gates_t = _inproj(x2d, seq, norm_mix_w, w_in, conv_w, dn_a_log, dn_dt_bias, tm)
    o_f, o_b = _delta_scan(qkv.reshape(bsz, seq, -1), gates_t, cb)
    x2d = _mix(x2d, o_f.reshape(t, DN_W), o_b.reshape(t, DN_W), dng, sgu, sgv, ga, gb, dn_norm_w, w_up_a,
               sg_ln_w, sg_ln_b, sg_w, sg_b, w_up_b, w_out, tm)
    n_mem = mem.shape[1]
    kv = _norm_proj(mem.reshape(bsz * n_mem, d), norm_mem_w, xa_w_kv, _largest_tile(bsz * n_mem, 256))
    x3d = _xattn(x2d.reshape(bsz, seq, d), kv.reshape(bsz, n_mem, 2 * d), norm_xa_w, xa_w_q, xa_w_o, tm)
    y = _ffn(x3d.reshape(t, d), norm_ffn_w, ffn_w_gate_up, ffn_w_down, final_norm_w, tm)
    return y.reshape(bsz, seq, d)


def kernel(x_prompt, x_sample, mem_prompt, mem_sample, norm_mix_w, w_in, conv_w, dn_a_log, dn_dt_bias, dn_norm_w, w_up_a, sg_ln_w, sg_ln_b, sg_w, sg_b, w_up_b, w_out, norm_xa_w, norm_mem_w, xa_w_q, xa_w_kv, xa_w_o, norm_ffn_w, ffn_w_gate_up, ffn_w_down, final_norm_w):
    depth = w_in.shape[0]
    assert depth == 1, "the FFN kernel fuses the final norm, which assumes a single layer"
    layer = (norm_mix_w[0], w_in[0], conv_w[0], dn_a_log[0], dn_dt_bias[0], dn_norm_w[0], w_up_a[0],
             sg_ln_w[0], sg_ln_b[0], sg_w[0], sg_b[0], w_up_b[0], w_out[0], norm_xa_w[0], norm_mem_w[0],
             xa_w_q[0], xa_w_kv[0], xa_w_o[0], norm_ffn_w[0], ffn_w_gate_up[0], ffn_w_down[0], final_norm_w)
    y_prompt = _trunk(x_prompt, mem_prompt, *layer)
    y_sample = _trunk(x_sample, mem_sample, *layer)
    return (y_prompt, y_sample)
```

```python
import functools

import jax
import jax.numpy as jnp
from jax import lax
from jax.experimental import pallas as pl
from jax.experimental.pallas import tpu as pltpu

F32 = jnp.float32
BF16 = jnp.bfloat16
EPS = 1e-6
LOG2_E = 1.4426950408889634

DN_HEADS = 4
DN_D = 128
DN_W = DN_HEADS * DN_D
DN_CHUNK = 64
CONV_K = 5
SG_GROUPS = 4
SG_CHUNK = 128
SG_W = SG_GROUPS * SG_CHUNK
XA_HEADS = 4

CONV_HALO_ROWS = 16
ROW_BLOCK = 64
DELTA_GROUP_STAGGER = 2
MIXATTN_STAGGER = 2
POST_TILE = 256
VMEM_LIMIT_BYTES = 56 * 1024 * 1024


def _dot(a, b):
    return jnp.dot(a.astype(BF16), b.astype(BF16), preferred_element_type=F32)


def _dot_nt(a, b):
    return lax.dot_general(a.astype(BF16), b.astype(BF16), (((1,), (1,)), ((), ())),
                           preferred_element_type=F32)


def _dot_tn(a, b):
    return lax.dot_general(a.astype(BF16), b.astype(BF16), (((0,), (0,)), ((), ())),
                           preferred_element_type=F32)


def _split_bf16(x):
    hi = x.astype(BF16)
    r = x - hi.astype(F32)
    mid = r.astype(BF16)
    lo = (r - mid.astype(F32)).astype(BF16)
    return hi, mid, lo


def _cumsum_rows(mask01, x):
    hi, mid, lo = _split_bf16(x)
    return _dot(mask01, hi) + (_dot(mask01, mid) + _dot(mask01, lo))


def _cumsum_lanes(x, mask01):
    m = x.shape[0]
    parts = _dot(jnp.concatenate(_split_bf16(x), axis=0), mask01)
    return parts[:m] + (parts[m:2 * m] + parts[2 * m:])


def _zero_after(x):
    bits = pltpu.bitcast(x[0:8, 0:128], jnp.uint32)
    sixteen = jnp.uint32(16)
    return lax.shift_right_logical(lax.shift_right_logical(bits, sixteen), sixteen).astype(F32)


def _rms(x, w):
    return x * lax.rsqrt(jnp.mean(x * x, -1, keepdims=True) + EPS) * w


def _silu(x):
    h = 0.5 * x
    return h + h * jnp.tanh(h)


def _gelu_tanh(x):
    c = 0.7978845608028654
    h = 0.5 * x
    return h + h * jnp.tanh(x * (c + (c * 0.044715) * (x * x)))


def _sigmoid_mix(ga, ya, gb, yb):
    return 0.5 * ((ya + yb) + (jnp.tanh(0.5 * ga) * ya + jnp.tanh(0.5 * gb) * yb))


def _params(*sem):
    return pltpu.CompilerParams(dimension_semantics=sem, vmem_limit_bytes=VMEM_LIMIT_BYTES)


def _const_spec(shape):
    nd = len(shape)
    return pl.BlockSpec(shape, lambda *_: (0,) * nd)


def _gate_math(ab, a_log, dt_bias, is_beta):
    z = ab + dt_bias
    softplus = jnp.maximum(z, 0.0) + jnp.log1p(jnp.exp(-jnp.abs(z)))
    return jnp.where(is_beta, jax.nn.sigmoid(ab), -jnp.exp(a_log) * softplus)


def _inproj_kernel(x0_ref, xnext_ref, xp_ref, xn_ref, nw_ref, wqkv_ref, cw_ref, wdng_ref, wsgu_ref, wsgv_ref,
                   wga_ref, wgb_ref, wabt_ref, alog_ref, dtb_ref,
                   qkv_ref, dng_ref, sgu_ref, sgv_ref, ga_ref, gb_ref, gatet_ref, raw_ref, h_ref, hnext_ref,
                   *, tiles_per_seq):
    tm = hnext_ref.shape[0]
    h0 = CONV_HALO_ROWS
    pad = CONV_K // 2
    mm_cols = 256
    pos = pl.program_id(0) % tiles_per_seq
    half = tm // 2

    @pl.when(pl.program_id(0) == 0)
    def _():
        hnext_ref[...] = _rms(x0_ref[...], nw_ref[...]).astype(BF16)

    h_ref[h0:h0 + tm, :] = hnext_ref[...]
    h_ref[0:h0, :] = _rms(xp_ref[...], nw_ref[...]).astype(BF16)
    h_ref[h0 + tm:2 * h0 + tm, :] = _rms(xn_ref[...], nw_ref[...]).astype(BF16)
    hs = {r0: h_ref[h0 + r0:h0 + r0 + half, :] for r0 in (0, half)}
    keep_prev = jnp.where(pos > 0, 1.0, 0.0)
    keep_next = jnp.where(pos < tiles_per_seq - 1, 1.0, 0.0)

    def qkv_chunk(j):
        res = _dot(h_ref[...], wqkv_ref[:, j:j + mm_cols])
        raw_ref[:, j:j + mm_cols] = res
        raw_ref[0:h0, j:j + mm_cols] = res[0:h0] * keep_prev
        raw_ref[h0 + tm:2 * h0 + tm, j:j + mm_cols] = res[h0 + tm:2 * h0 + tm] * keep_next

    def conv_unit(g, r, after):
        cols = slice(g * DN_D, (g + 1) * DN_D)
        lo = h0 - pad + r
        acc = raw_ref[lo:lo + ROW_BLOCK, cols] * (cw_ref[0:1, cols] + after[0:1, :])
        for j in range(1, CONV_K):
            acc = acc + raw_ref[lo + j:lo + j + ROW_BLOCK, cols] * cw_ref[j:j + 1, cols]
        y = _silu(acc)
        if g < 2 * DN_HEADS:
            y = y * lax.rsqrt(jnp.sum(y * y, -1, keepdims=True) + EPS)
        if g < DN_HEADS:
            y = y * (DN_D ** -0.5)
        qkv_ref[r:r + ROW_BLOCK, cols] = y.astype(BF16)

    rest = [(w_ref, o_ref, j, r0)
            for w_ref, o_ref in ((wdng_ref, dng_ref), (wsgu_ref, sgu_ref), (wsgv_ref, sgv_ref),
                                 (wga_ref, ga_ref), (wgb_ref, gb_ref))
            for j in range(0, w_ref.shape[1], mm_cols) for r0 in (0, half)]
    n_rest = len(rest)

    def rest_unit(w_ref, o_ref, j, r0):
        res = _dot(hs[r0], w_ref[:, j:j + mm_cols])
        o_ref[r0:r0 + half, j:j + mm_cols] = res.astype(o_ref.dtype)
        return _zero_after(res)

    groups_per_chunk = mm_cols // DN_D
    n_chunks = 3 * DN_W // mm_cols
    n_conv = 3 * DN_HEADS * (tm // ROW_BLOCK)
    done = 0
    after = jnp.zeros((8, DN_D), F32)
    norm_rows = ROW_BLOCK // 2
    norm_units = list(range(0, tm, norm_rows))
    n_norm = len(norm_units)

    def norm_unit(r, after):
        nw = nw_ref[...] + jnp.concatenate([after[0:1, :]] * (nw_ref.shape[1] // DN_D), axis=1)
        hnext_ref[r:r + norm_rows, :] = _rms(xnext_ref[r:r + norm_rows, :], nw).astype(BF16)

    qkv_chunk(0)
    for ch in range(n_chunks):
        if ch + 1 < n_chunks:
            qkv_chunk((ch + 1) * mm_cols)
        for g in range(ch * groups_per_chunk, (ch + 1) * groups_per_chunk):
            for r in range(0, tm, ROW_BLOCK):
                conv_unit(g, r, after)
                done += 1
                while rest and (n_rest - len(rest)) * n_conv < done * n_rest:
                    after = rest_unit(*rest.pop(0))
                while norm_units and (n_norm - len(norm_units)) * n_conv < done * n_norm:
                    norm_unit(norm_units.pop(0), after)
    while rest:
        rest_unit(*rest.pop(0))
    abt = _dot_nt(wabt_ref[...], h_ref[h0:h0 + tm, :])
    row = lax.broadcasted_iota(jnp.int32, abt.shape, 0)
    gatet_ref[...] = _gate_math(abt, alog_ref[...], dtb_ref[...], row < 2 * DN_HEADS)


def _inproj(x2d, seq, norm_w, w_in, conv_w, a_log, dt_bias, tm):
    t, d = x2d.shape
    hb = tm // CONV_HALO_ROWS
    last_halo = t // CONV_HALO_ROWS - 1
    n_qkv = 3 * DN_W
    n_ab = 4 * DN_HEADS
    bounds = [0, n_qkv, n_qkv + n_ab]
    for width in (DN_W, SG_W, SG_W, d, d):
        bounds.append(bounds[-1] + width)
    assert bounds[-1] == w_in.shape[1]
    piece = lambda k: w_in[:, bounds[k]:bounds[k + 1]].astype(BF16)
    w_qkv, w_ab = piece(0), piece(1)
    w_rest = [piece(k) for k in range(2, 7)]
    zeros8 = jnp.zeros((2 * DN_HEADS,), F32)
    alog16 = jnp.concatenate([zeros8, a_log.reshape(-1).astype(F32)])
    dtb16 = jnp.concatenate([zeros8, dt_bias.reshape(-1).astype(F32)])
    widths = [n_qkv, DN_W, SG_W, SG_W, d, d]
    out_shape = [jax.ShapeDtypeStruct((t, n), BF16) for n in widths]
    out_shape += [jax.ShapeDtypeStruct((n_ab, t), F32)]
    out_specs = [pl.BlockSpec((tm, n), lambda i: (i, 0)) for n in widths]
    out_specs += [pl.BlockSpec((n_ab, tm), lambda i: (0, i))]
    consts = [norm_w.reshape(1, d).astype(F32), w_qkv, conv_w.astype(F32), *w_rest, w_ab.T,
              alog16.reshape(n_ab, 1), dtb16.reshape(n_ab, 1)]
    n_tiles = t // tm
    x_specs = [pl.BlockSpec((tm, d), lambda i: (0, 0)),
               pl.BlockSpec((tm, d), lambda i: (jnp.minimum(i + 1, n_tiles - 1), 0)),
               pl.BlockSpec((CONV_HALO_ROWS, d), lambda i: (jnp.maximum(i * hb - 1, 0), 0)),
               pl.BlockSpec((CONV_HALO_ROWS, d), lambda i: (jnp.minimum((i + 1) * hb, last_halo), 0))]
    return pl.pallas_call(
        functools.partial(_inproj_kernel, tiles_per_seq=seq // tm),
        grid=(n_tiles,),
        in_specs=x_specs + [_const_spec(c.shape) for c in consts],
        out_specs=out_specs,
        out_shape=out_shape,
        scratch_shapes=[pltpu.VMEM((tm + 2 * CONV_HALO_ROWS, n_qkv), F32),
                        pltpu.VMEM((tm + 2 * CONV_HALO_ROWS, d), BF16), pltpu.VMEM((tm, d), BF16)],
        compiler_params=_params("arbitrary"),
        name="inproj",
    )(x2d, x2d, x2d, x2d, *consts)


def _delta_kernel(qf_ref, gtf_ref, qb_ref, gtb_ref, of_ref, ob_ref, state_ref, *, cb):
    c = DN_CHUNK
    nc = cb // c
    qkv_refs = (qf_ref, qb_ref)
    n_gate = gtf_ref.shape[0]

    @pl.when(pl.program_id(1) == 0)
    def _():
        state_ref[...] = jnp.zeros_like(state_ref)

    ri = lax.broadcasted_iota(jnp.int32, (cb, cb), 0)
    ci = lax.broadcasted_iota(jnp.int32, (cb, cb), 1)
    same = (ri & -c) == (ci & -c)
    ge = jnp.where(same & (ri >= ci), 1.0, 0.0)
    le = jnp.where(same & (ri <= ci), 1.0, 0.0)
    gates_t = (gtf_ref[...], gtb_ref[...])
    gc_t = (_cumsum_lanes(gates_t[0], le) * LOG2_E, _cumsum_lanes(gates_t[1], ge) * LOG2_E)
    is_beta = lax.broadcasted_iota(jnp.int32, (n_gate, cb), 0) < 2 * DN_HEADS
    pad_rows = jnp.zeros((DN_D - n_gate, cb), F32)
    gc = tuple(jnp.concatenate([jnp.where(is_beta, gates_t[d], gc_t[d]), pad_rows], axis=0).T for d in range(2))
    gates = gc

    ri2 = lax.broadcasted_iota(jnp.int32, (c, 2 * c), 0)
    lane2 = lax.broadcasted_iota(jnp.int32, (c, 2 * c), 1)
    ci2 = lane2 & (c - 1)
    left = lane2 < c
    keep_a = (left & (ri2 <= ci2), left & (ri2 >= ci2))
    keep_l = (~left & (ri2 > ci2), ~left & (ri2 < ci2))
    sign = jnp.where(left, -1.0, 1.0)
    eye2 = jnp.where(ri2 == ci2, 1.0, 0.0)
    zeros2 = jnp.zeros((c, 2 * c), BF16)

    state = {(d, h): state_ref[d, h] for d in range(2) for h in range(DN_HEADS)}
    steps_done = [0]
    o_refs = (of_ref, ob_ref)

    def group(s):
        items = [(d, (nc - 1 - s) if d else s, h) for d in range(2) for h in range(DN_HEADS)]
        q, k, v, beta, g_col, g_last, decay = {}, {}, {}, {}, {}, {}, {}
        for it in items:
            d, n, h = it
            rows = slice(n * c, (n + 1) * c)
            q[it] = qkv_refs[d][rows, h * DN_D:(h + 1) * DN_D]
            k[it] = qkv_refs[d][rows, DN_W + h * DN_D:DN_W + (h + 1) * DN_D]
            v[it] = qkv_refs[d][rows, 2 * DN_W + h * DN_D:2 * DN_W + (h + 1) * DN_D]
        kqk = {it: _dot_nt(k[it], jnp.concatenate([q[it], k[it]], axis=0)) for it in items}
        yield
        for it in items:
            d, n, h = it
            rows = slice(n * c, (n + 1) * c)
            bcol = d * DN_HEADS + h
            gcol = (2 + d) * DN_HEADS + h
            beta[it] = jnp.broadcast_to(gates[d][rows, bcol:bcol + 1], (c, DN_D))
            g_col[it] = jnp.broadcast_to(gc[d][rows, gcol:gcol + 1], (c, DN_D))
            g_row = gc_t[d][gcol:gcol + 1, rows]
            g_last[it] = g_col[it][0:1, :] if d else g_col[it][c - 1:c, :]
            decay[it] = jnp.exp2((g_col[it] - jnp.concatenate([g_row, g_row], axis=1)) * sign)
        yield
        kqd = {it: kqk[it] * decay[it] for it in items}
        a_intra_t = {it: jnp.where(keep_a[it[0]], kqd[it], 0.0)[:, :c] for it in items}
        lm_r = {it: jnp.where(keep_l[it[0]], beta[it] * kqd[it], 0.0) for it in items}
        ll = {it: lm_r[it] + pltpu.roll(lm_r[it], c, 1) for it in items}
        sq = {it: _dot(ll[it], jnp.concatenate([lm_r[it].astype(BF16), zeros2], axis=0)) for it in items}
        yield
        z = {it: jnp.where(left, eye2 - ll[it], sq[it]) for it in items}
        eg = {it: jnp.exp2(g_col[it]) for it in items}
        uw_rhs = {it: jnp.concatenate([(v[it].astype(F32) * beta[it]).astype(BF16),
                                       (k[it].astype(F32) * (beta[it] * eg[it])).astype(BF16)], axis=1)
                  for it in items}
        uw_rhs = {it: jnp.concatenate([uw_rhs[it], jnp.zeros((c, 2 * DN_D), BF16)], axis=0) for it in items}
        power = 2
        while power < c:
            zb = {it: z[it].astype(BF16) for it in items}
            r = {it: _dot(zb[it], jnp.concatenate([zeros2, zb[it]], axis=0)) for it in items}
            yield
            keep = left if 2 * power < c else True
            z = {it: jnp.where(keep, z[it], 0.0) + r[it] for it in items}
            power *= 2
        uw = {it: _dot(z[it], uw_rhs[it]) for it in items}
        kga = {it: jnp.concatenate([(k[it].astype(F32) * jnp.exp2(g_last[it] - g_col[it])).astype(BF16),
                                    a_intra_t[it].astype(BF16)], axis=1) for it in items}
        yield
        assert steps_done[0] == s, "recurrence steps must be issued in order"
        wq = {it: jnp.concatenate([uw[it][:, DN_D:], q[it].astype(F32) * eg[it]], axis=0) for it in items}
        ws_qs = {it: _dot(wq[it], state[it[0], it[2]]) for it in items}
        yield
        v_new = {it: uw[it][:, :DN_D] - ws_qs[it][:c] for it in items}
        upd = {it: _dot_tn(kga[it], v_new[it]) for it in items}
        yield
        for it in items:
            d, n, h = it
            state[d, h] = state[d, h] * jnp.exp2(g_last[it]) + upd[it][:DN_D]
            o = ws_qs[it][c:] + upd[it][DN_D:]
            o_refs[d][n * c:(n + 1) * c, h * DN_D:(h + 1) * DN_D] = o.astype(BF16)
        steps_done[0] += 1

    waiting = [group(s) for s in range(nc)]
    live = []
    tick = 0
    while waiting or live:
        if waiting and tick % DELTA_GROUP_STAGGER == 0:
            live.append(waiting.pop(0))
        for g in list(live):
            try:
                next(g)
            except StopIteration:
                live.remove(g)
        tick += 1
    for (d, h), val in state.items():
        state_ref[d, h] = val


def _delta_scan(qkv, gates_t, cb):
    bsz, seq, n_qkv = qkv.shape
    nb = seq // cb
    n_gate = gates_t.shape[0]

    def stream_specs(blk):
        return [
            pl.BlockSpec((None, cb, n_qkv), lambda b, i: (b, blk(i), 0)),
            pl.BlockSpec((n_gate, cb), lambda b, i: (0, b * nb + blk(i))),
        ]

    fwd = lambda i: i
    bwd = lambda i: nb - 1 - i
    stream = (qkv, gates_t)
    return pl.pallas_call(
        functools.partial(_delta_kernel, cb=cb),
        grid=(bsz, nb),
        in_specs=stream_specs(fwd) + stream_specs(bwd),
        out_specs=[pl.BlockSpec((None, cb, DN_W), lambda b, i: (b, fwd(i), 0)),
                   pl.BlockSpec((None, cb, DN_W), lambda b, i: (b, bwd(i), 0))],
        out_shape=[jax.ShapeDtypeStruct((bsz, seq, DN_W), BF16)] * 2,
        scratch_shapes=[pltpu.VMEM((2, DN_HEADS, DN_D, DN_D), F32)],
        compiler_params=_params("parallel", "arbitrary"),
        name="delta",
    )(*stream, *stream)


def _mix_kernel(x_ref, of_ref, ob_ref, dng_ref, sgu_ref, sgv_ref, ga_ref, gb_ref,
                dnw_ref, wua_ref, lnw_ref, lnb_ref, ws_ref, bst_ref, wub_ref, wout_ref,
                y_ref, dn_s, sg_s):
    tm = x_ref.shape[0]
    for h in range(DN_HEADS):
        cols = slice(h * DN_D, (h + 1) * DN_D)
        o = of_ref[:, cols].astype(F32) + ob_ref[:, cols].astype(F32)
        dn_s[:, cols] = (_rms(o, dnw_ref[...]) * _silu(dng_ref[:, cols].astype(F32))).astype(BF16)
    y_a = _dot(dn_s[...], wua_ref[...])
    v = _gelu_tanh(sgv_ref[...].astype(F32))
    mu = jnp.mean(v, -1, keepdims=True)
    vc = v - mu
    v = vc * lax.rsqrt(jnp.mean(vc * vc, -1, keepdims=True) + EPS) * lnw_ref[...] + lnb_ref[...]
    v = v.astype(BF16)
    for n in range(tm // SG_CHUNK):
        rows = slice(n * SG_CHUNK, (n + 1) * SG_CHUNK)
        for g in range(SG_GROUPS):
            cols = slice(g * SG_CHUNK, (g + 1) * SG_CHUNK)
            mixed = _dot(ws_ref[g], v[rows, cols]) + bst_ref[:, g:g + 1]
            sg_s[rows, cols] = (_gelu_tanh(sgu_ref[rows, cols].astype(F32)) * mixed).astype(BF16)
    y_b = _dot(sg_s[...], wub_ref[...])
    merged = _sigmoid_mix(ga_ref[...].astype(F32), y_a, gb_ref[...].astype(F32), y_b)
    y_ref[...] = x_ref[...] + _dot(merged, wout_ref[...])


def _mix(x2d, o_f, o_b, dng, sgu, sgv, ga, gb, dn_norm_w, w_up_a, ln_w, ln_b, sg_w, sg_b, w_up_b, w_out, tm):
    t, d = x2d.shape
    consts = [dn_norm_w.reshape(1, DN_D).astype(F32), w_up_a.astype(BF16),
              ln_w.reshape(1, SG_W).astype(F32), ln_b.reshape(1, SG_W).astype(F32),
              sg_w.astype(BF16), sg_b.T.astype(F32), w_up_b.astype(BF16), w_out.astype(BF16)]
    acts = [x2d, o_f, o_b, dng, sgu, sgv, ga, gb]
    return pl.pallas_call(
        _mix_kernel,
        grid=(t // tm,),
        in_specs=[pl.BlockSpec((tm, a.shape[1]), lambda i: (i, 0)) for a in acts]
                 + [_const_spec(c.shape) for c in consts],
        out_specs=pl.BlockSpec((tm, d), lambda i: (i, 0)),
        out_shape=jax.ShapeDtypeStruct((t, d), F32),
        scratch_shapes=[pltpu.VMEM((tm, DN_W), BF16), pltpu.VMEM((tm, SG_W), BF16)],
        compiler_params=_params("parallel"),
        name="mix",
    )(*acts, *consts)


def _norm_proj_kernel(x_ref, nw_ref, w_ref, o_ref):
    o_ref[...] = _dot(_rms(x_ref[...], nw_ref[...]), w_ref[...]).astype(o_ref.dtype)


def _norm_proj(x2d, norm_w, w, tm):
    t, d = x2d.shape
    n = w.shape[1]
    return pl.pallas_call(
        _norm_proj_kernel,
        grid=(t // tm,),
        in_specs=[pl.BlockSpec((tm, d), lambda i: (i, 0)), _const_spec((1, d)), _const_spec(w.shape)],
        out_specs=pl.BlockSpec((tm, n), lambda i: (i, 0)),
        out_shape=jax.ShapeDtypeStruct((t, n), BF16),
        compiler_params=_params("parallel"),
        name="mem_kv",
    )(x2d, norm_w.reshape(1, d).astype(F32), w.astype(BF16))


def _xattn_kernel(x_ref, kv_ref, nw_ref, wq_ref, wo_ref, y_ref, o_s):
    d = x_ref.shape[1]
    hd = d // XA_HEADS
    x = x_ref[...]
    q = _dot(_rms(x, nw_ref[...]), wq_ref[...])
    for h in range(XA_HEADS):
        cols = slice(h * hd, (h + 1) * hd)
        s = _dot_nt(q[:, cols], kv_ref[:, cols]) * (hd ** -0.5)
        p = jnp.exp(s - jnp.max(s, -1, keepdims=True))
        p = p / jnp.sum(p, -1, keepdims=True)
        o_s[:, cols] = _dot(p, kv_ref[:, d + h * hd:d + (h + 1) * hd]).astype(BF16)
    y_ref[...] = x + _dot(o_s[...], wo_ref[...])


def _xattn(x3d, kv, norm_w, w_q, w_o, tm):
    bsz, seq, d = x3d.shape
    n_mem = kv.shape[1]
    consts = [norm_w.reshape(1, d).astype(F32), w_q.astype(BF16), w_o.astype(BF16)]
    return pl.pallas_call(
        _xattn_kernel,
        grid=(bsz, seq // tm),
        in_specs=[pl.BlockSpec((None, tm, d), lambda b, i: (b, i, 0)),
                  pl.BlockSpec((None, n_mem, 2 * d), lambda b, i: (b, 0, 0))]
                 + [_const_spec(c.shape) for c in consts],
        out_specs=pl.BlockSpec((None, tm, d), lambda b, i: (b, i, 0)),
        out_shape=jax.ShapeDtypeStruct((bsz, seq, d), F32),
        scratch_shapes=[pltpu.VMEM((tm, d), BF16)],
        compiler_params=_params("parallel", "parallel"),
        name="xattn",
    )(x3d, kv, *consts)


def _run_staggered(gens, stagger):
    waiting = list(gens)
    live = []
    tick = 0
    while waiting or live:
        if waiting and tick % stagger == 0:
            live.append(waiting.pop(0))
        for g in list(live):
            try:
                next(g)
            except StopIteration:
                live.remove(g)
        tick += 1


def _mixattn_kernel(x_ref, of_ref, ob_ref, dng_ref, sgu_ref, sgv_ref, ga_ref, gb_ref, kv_ref,
                    dnw_ref, wua_ref, lnw_ref, lnb_ref, ws_ref, bst_ref, wub_ref, wout_ref,
                    xnw_ref, wq_ref, wo_ref, y_ref):
    tm, d = x_ref.shape
    hd = d // XA_HEADS
    half = tm // 2
    pace = [jnp.zeros((8, DN_D), F32)]

    def paced_row(ref):
        n = ref.shape[1] // DN_D
        return ref[...] + jnp.concatenate([pace[0][0:1, :]] * n, axis=1)

    def sub_tile(r0):
        rows = slice(r0, r0 + half)
        dnw = paced_row(dnw_ref)
        dn = []
        for h in range(DN_HEADS):
            cols = slice(h * DN_D, (h + 1) * DN_D)
            o = of_ref[rows, cols].astype(F32) + ob_ref[rows, cols].astype(F32)
            dn.append((_rms(o, dnw) * _silu(dng_ref[rows, cols].astype(F32))).astype(BF16))
        yield
        lnb = paced_row(lnb_ref)
        y_a = _dot(jnp.concatenate(dn, axis=1), wua_ref[...])
        pace[0] = _zero_after(y_a)
        v = _gelu_tanh(sgv_ref[rows, :].astype(F32))
        vc = v - jnp.mean(v, -1, keepdims=True)
        v = (vc * lax.rsqrt(jnp.mean(vc * vc, -1, keepdims=True) + EPS) * lnw_ref[...] + lnb).astype(BF16)
        yield
        sg = []
        for n in range(half // SG_CHUNK):
            blk = slice(r0 + n * SG_CHUNK, r0 + (n + 1) * SG_CHUNK)
            sg_n = []
            for g in range(SG_GROUPS):
                cols = slice(g * SG_CHUNK, (g + 1) * SG_CHUNK)
                mixed = _dot(ws_ref[g], v[n * SG_CHUNK:(n + 1) * SG_CHUNK, cols]) + bst_ref[:, g:g + 1]
                sg_n.append((_gelu_tanh(sgu_ref[blk, cols].astype(F32)) * mixed).astype(BF16))
            sg.append(jnp.concatenate(sg_n, axis=1))
        yield
        y_b = _dot(jnp.concatenate(sg, axis=0), wub_ref[...])
        merged = _sigmoid_mix(ga_ref[rows, :].astype(F32), y_a, gb_ref[rows, :].astype(F32), y_b)
        yield
        x1 = x_ref[rows, :] + _dot(merged, wout_ref[...])
        pace[0] = _zero_after(x1)
        yield
        q = _dot(_rms(x1, xnw_ref[...]), wq_ref[...])
        pace[0] = _zero_after(q)
        yield
        o = []
        for h in range(XA_HEADS):
            cols = slice(h * hd, (h + 1) * hd)
            s = _dot_nt(q[:, cols], kv_ref[:, cols]) * (hd ** -0.5)
            p = jnp.exp(s - jnp.max(s, -1, keepdims=True))
            p = p / jnp.sum(p, -1, keepdims=True)
            o.append(_dot(p, kv_ref[:, d + h * hd:d + (h + 1) * hd]).astype(BF16))
        yield
        y = x1 + _dot(jnp.concatenate(o, axis=1), wo_ref[...])
        pace[0] = _zero_after(y)
        y_ref[rows, :] = y

    _run_staggered([sub_tile(0), sub_tile(half)], MIXATTN_STAGGER)


def _mixattn(x3d, o_f, o_b, dng, sgu, sgv, ga, gb, kv, dn_norm_w, w_up_a, ln_w, ln_b, sg_w, sg_b, w_up_b, w_out,
             xa_norm_w, w_q, w_o, tm):
    bsz, seq, d = x3d.shape
    n_mem = kv.shape[1]
    consts = [dn_norm_w.reshape(1, DN_D).astype(F32), w_up_a.astype(BF16),
              ln_w.reshape(1, SG_W).astype(F32), ln_b.reshape(1, SG_W).astype(F32),
              sg_w.astype(BF16), sg_b.T.astype(F32), w_up_b.astype(BF16), w_out.astype(BF16),
              xa_norm_w.reshape(1, d).astype(F32), w_q.astype(BF16), w_o.astype(BF16)]
    acts = [x3d] + [a.reshape(bsz, seq, -1) for a in (o_f, o_b, dng, sgu, sgv, ga, gb)]
    return pl.pallas_call(
        _mixattn_kernel,
        grid=(bsz, seq // tm),
        in_specs=[pl.BlockSpec((None, tm, a.shape[2]), lambda b, i: (b, i, 0)) for a in acts]
                 + [pl.BlockSpec((None, n_mem, 2 * d), lambda b, i: (b, 0, 0))]
                 + [_const_spec(c.shape) for c in consts],
        out_specs=pl.BlockSpec((None, tm, d), lambda b, i: (b, i, 0)),
        out_shape=jax.ShapeDtypeStruct((bsz, seq, d), F32),
        compiler_params=_params("parallel", "parallel"),
        name="mixattn",
    )(*acts, kv, *consts)


def _ffn_kernel(x_ref, nw_ref, wg_ref, wu_ref, wd_ref, fw_ref, y_ref, *, ff_chunks):
    x = x_ref[...]
    h = _rms(x, nw_ref[...]).astype(BF16)
    acc = x
    for lo, hi in ff_chunks:
        a = _silu(_dot(h, wg_ref[:, lo:hi])) * _dot(h, wu_ref[:, lo:hi])
        acc = acc + _dot(a, wd_ref[lo:hi, :])
    y_ref[...] = _rms(acc, fw_ref[...])


def _ffn(x2d, norm_w, w_gate_up, w_down, final_w, tm):
    t, d = x2d.shape
    d_ff = w_down.shape[0]
    mxu_cols = 256
    assert d_ff % mxu_cols == 0
    half = (d_ff // mxu_cols + 1) // 2 * mxu_cols
    ff_chunks = ((0, half), (half, d_ff))
    consts = [norm_w.reshape(1, d).astype(F32), w_gate_up[:, :d_ff].astype(BF16),
              w_gate_up[:, d_ff:].astype(BF16), w_down.astype(BF16), final_w.reshape(1, d).astype(F32)]
    return pl.pallas_call(
        functools.partial(_ffn_kernel, ff_chunks=ff_chunks),
        grid=(t // tm,),
        in_specs=[pl.BlockSpec((tm, d), lambda i: (i, 0))] + [_const_spec(c.shape) for c in consts],
        out_specs=pl.BlockSpec((tm, d), lambda i: (i, 0)),
        out_shape=jax.ShapeDtypeStruct((t, d), F32),
        compiler_params=_params("parallel"),
        name="ffn",
    )(x2d, *consts)


def _post_kernel(x_ref, of_ref, ob_ref, dng_ref, sgu_ref, sgv_ref, ga_ref, gb_ref, kv_ref,
                 dnw_ref, wua_ref, lnw_ref, lnb_ref, ws_ref, bst_ref, wub_ref, wout_ref,
                 xnw_ref, wq_ref, wo_ref, fnw_ref, wg_ref, wu_ref, wd_ref, fw_ref,
                 y_ref, x2_s, *, ff_chunks):
    tm, d = x_ref.shape
    hd = d // XA_HEADS
    i = pl.program_id(0)
    slot = i % 2

    @pl.when(i == 0)
    def _():
        x2_s[...] = jnp.zeros_like(x2_s)

    pace = [jnp.zeros((8, DN_D), F32)]

    def paced_row(ref):
        n = ref.shape[1] // DN_D
        return ref[...] + jnp.concatenate([pace[0][0:1, :]] * n, axis=1)

    def ffn_stream():
        x2 = x2_s[1 - slot]
        h = _rms(x2, fnw_ref[...]).astype(BF16)
        acc = x2
        yield
        for lo, hi in ff_chunks:
            a = _silu(_dot(h, wg_ref[:, lo:hi])) * _dot(h, wu_ref[:, lo:hi])
            acc = acc + _dot(a, wd_ref[lo:hi, :])
            pace[0] = _zero_after(acc)
            yield
        y_ref[...] = _rms(acc, fw_ref[...])

    def mix_stream():
        dnw = paced_row(dnw_ref)
        dn = []
        for h in range(DN_HEADS):
            cols = slice(h * DN_D, (h + 1) * DN_D)
            o = of_ref[:, cols].astype(F32) + ob_ref[:, cols].astype(F32)
            dn.append((_rms(o, dnw) * _silu(dng_ref[:, cols].astype(F32))).astype(BF16))
        yield
        lnb = paced_row(lnb_ref)
        y_a = _dot(jnp.concatenate(dn, axis=1), wua_ref[...])
        v = _gelu_tanh(sgv_ref[...].astype(F32))
        vc = v - jnp.mean(v, -1, keepdims=True)
        v = (vc * lax.rsqrt(jnp.mean(vc * vc, -1, keepdims=True) + EPS) * lnw_ref[...] + lnb).astype(BF16)
        yield
        bst = bst_ref[...]
        sg = []
        for n in range(tm // SG_CHUNK):
            blk = slice(n * SG_CHUNK, (n + 1) * SG_CHUNK)
            sg_n = []
            for g in range(SG_GROUPS):
                cols = slice(g * SG_CHUNK, (g + 1) * SG_CHUNK)
                mixed = _dot(ws_ref[g], v[blk, cols]) + bst[:, g:g + 1]
                sg_n.append((_gelu_tanh(sgu_ref[blk, cols].astype(F32)) * mixed).astype(BF16))
            sg.append(jnp.concatenate(sg_n, axis=1))
        yield
        y_b = _dot(jnp.concatenate(sg, axis=0), wub_ref[...])
        merged = _sigmoid_mix(ga_ref[...].astype(F32), y_a, gb_ref[...].astype(F32), y_b)
        yield
        x1 = x_ref[...] + _dot(merged, wout_ref[...])
        yield
        q = _dot(_rms(x1, xnw_ref[...]), wq_ref[...])
        yield
        o = []
        for h in range(XA_HEADS):
            cols = slice(h * hd, (h + 1) * hd)
            s = _dot_nt(q[:, cols], kv_ref[:, cols]) * (hd ** -0.5)
            p = jnp.exp(s - jnp.max(s, -1, keepdims=True))
            p = p / jnp.sum(p, -1, keepdims=True)
            o.append(_dot(p, kv_ref[:, d + h * hd:d + (h + 1) * hd]).astype(BF16))
        yield
        x2_s[slot] = x1 + _dot(jnp.concatenate(o, axis=1), wo_ref[...])

    _run_staggered([ffn_stream(), mix_stream()], 1)


def _post(x2d, seq, o_f, o_b, dng, sgu, sgv, ga, gb, kv, dn_norm_w, w_up_a, ln_w, ln_b, sg_w, sg_b, w_up_b, w_out,
          xa_norm_w, w_q, w_o, ffn_norm_w, w_gate_up, w_down, final_w, tm):
    t, d = x2d.shape
    n_tiles = t // tm
    tiles_per_seq = seq // tm
    n_mem = kv.shape[1]
    d_ff = w_down.shape[0]
    ff_step = 512
    ff_chunks = tuple((lo, min(lo + ff_step, d_ff)) for lo in range(0, d_ff, ff_step))
    consts = [dn_norm_w.reshape(1, DN_D).astype(F32), w_up_a.astype(BF16),
              ln_w.reshape(1, SG_W).astype(F32), ln_b.reshape(1, SG_W).astype(F32),
              sg_w.astype(BF16), sg_b.T.astype(F32), w_up_b.astype(BF16), w_out.astype(BF16),
              xa_norm_w.reshape(1, d).astype(F32), w_q.astype(BF16), w_o.astype(BF16),
              ffn_norm_w.reshape(1, d).astype(F32), w_gate_up[:, :d_ff].astype(BF16),
              w_gate_up[:, d_ff:].astype(BF16), w_down.astype(BF16), final_w.reshape(1, d).astype(F32)]
    acts = [x2d] + [a.reshape(t, -1) for a in (o_f, o_b, dng, sgu, sgv, ga, gb)]
    cur = lambda i: jnp.minimum(i, n_tiles - 1)
    prev = lambda i: jnp.maximum(i - 1, 0)
    return pl.pallas_call(
        functools.partial(_post_kernel, ff_chunks=ff_chunks),
        grid=(n_tiles + 1,),
        in_specs=[pl.BlockSpec((tm, a.shape[1]), lambda i: (cur(i), 0)) for a in acts]
                 + [pl.BlockSpec((None, n_mem, 2 * d), lambda i: (cur(i) // tiles_per_seq, 0, 0))]
                 + [_const_spec(c.shape) for c in consts],
        out_specs=pl.BlockSpec((tm, d), lambda i: (prev(i), 0)),
        out_shape=jax.ShapeDtypeStruct((t, d), F32),
        scratch_shapes=[pltpu.VMEM((2, tm, d), F32)],
        compiler_params=_params("arbitrary"),
        name="post",
    )(*acts, kv, *consts)


def _largest_tile(n, cap):
    t = cap
    while n % t:
        t //= 2
    return t


def _trunk(x, mem, norm_mix_w, w_in, conv_w, dn_a_log, dn_dt_bias, dn_norm_w, w_up_a, sg_ln_w, sg_ln_b,
           sg_w, sg_b, w_up_b, w_out, norm_xa_w, norm_mem_w, xa_w_q, xa_w_kv, xa_w_o, norm_ffn_w,
           ffn_w_gate_up, ffn_w_down, final_norm_w):
    bsz, seq, d = x.shape
    t = bsz * seq
    assert seq % SG_CHUNK == 0 and seq % DN_CHUNK == 0
    tm = _largest_tile(seq, 512)
    cb = _largest_tile(seq, 512)
    x2d = x.reshape(t, d)
    qkv, dng, sgu, sgv, ga, gb, gates_t = _inproj(x2d, seq, norm_mix_w, w_in, conv_w, dn_a_log, dn_dt_bias, tm)
    o_f, o_b = _delta_scan(qkv.reshape(bsz, seq, -1), gates_t, cb)
    n_mem = mem.shape[1]
    kv = _norm_proj(mem.reshape(bsz * n_mem, d), norm_mem_w, xa_w_kv, _largest_tile(bsz * n_mem, 256))
    x2d = _mix(x2d, o_f.reshape(t, DN_W), o_b.reshape(t, DN_W), dng, sgu, sgv, ga, gb, dn_norm_w, w_up_a,
               sg_ln_w, sg_ln_b, sg_w, sg_b, w_up_b, w_out, tm)
    x3d = _xattn(x2d.reshape(bsz, seq, d), kv.reshape(bsz, n_mem, 2 * d), norm_xa_w, xa_w_q, xa_w_o, tm)
    y = _ffn(x3d.reshape(t, d), norm_ffn_w, ffn_w_gate_up, ffn_w_down, final_norm_w, tm)
    return y.reshape(bsz, seq, d)


def kernel(x_prompt, x_sample, mem_prompt, mem_sample, norm_mix_w, w_in, conv_w, dn_a_log, dn_dt_bias, dn_norm_w, w_up_a, sg_ln_w, sg_ln_b, sg_w, sg_b, w_up_b, w_out, norm_xa_w, norm_mem_w, xa_w_q, xa_w_kv, xa_w_o, norm_ffn_w, ffn_w_gate_up, ffn_w_down, final_norm_w):
    depth = w_in.shape[0]
    assert depth == 1, "the FFN kernel fuses the final norm, which assumes a single layer"
    layer = (norm_mix_w[0], w_in[0], conv_w[0], dn_a_log[0], dn_dt_bias[0], dn_norm_w[0], w_up_a[0],
             sg_ln_w[0], sg_ln_b[0], sg_w[0], sg_b[0], w_up_b[0], w_out[0], norm_xa_w[0], norm_mem_w[0],
             xa_w_q[0], xa_w_kv[0], xa_w_o[0], norm_ffn_w[0], ffn_w_gate_up[0], ffn_w_down[0], final_norm_w)
    y_prompt = _trunk(x_prompt, mem_prompt, *layer)
    y_sample = _trunk(x_sample, mem_sample, *layer)
    return (y_prompt, y_sample)
```

```python
import functools

import jax
import jax.numpy as jnp
from jax import lax
from jax.experimental import pallas as pl
from jax.experimental.pallas import tpu as pltpu

F32 = jnp.float32
BF16 = jnp.bfloat16
EPS = 1e-6
LOG2_E = 1.4426950408889634

DN_HEADS = 4
DN_D = 128
DN_W = DN_HEADS * DN_D
DN_CHUNK = 64
CONV_K = 5
SG_GROUPS = 4
SG_CHUNK = 128
SG_W = SG_GROUPS * SG_CHUNK
XA_HEADS = 4

CONV_HALO_ROWS = 16
ROW_BLOCK = 64
DELTA_GROUP_DIRS = ((0, 1),)
DELTA_GROUP_STAGGER = 2
MIXATTN_STAGGER = 2
POST_TILE = 256
VMEM_LIMIT_BYTES = 56 * 1024 * 1024


def _dot(a, b):
    return jnp.dot(a.astype(BF16), b.astype(BF16), preferred_element_type=F32)


def _dot_nt(a, b):
    return lax.dot_general(a.astype(BF16), b.astype(BF16), (((1,), (1,)), ((), ())),
                           preferred_element_type=F32)


def _dot_tn(a, b):
    return lax.dot_general(a.astype(BF16), b.astype(BF16), (((0,), (0,)), ((), ())),
                           preferred_element_type=F32)


def _split_bf16(x):
    hi = x.astype(BF16)
    r = x - hi.astype(F32)
    mid = r.astype(BF16)
    lo = (r - mid.astype(F32)).astype(BF16)
    return hi, mid, lo


def _cumsum_rows(mask01, x):
    hi, mid, lo = _split_bf16(x)
    return _dot(mask01, hi) + (_dot(mask01, mid) + _dot(mask01, lo))


def _cumsum_lanes(x, mask01):
    m, n = x.shape
    w = mask01.shape[0]
    parts = jnp.concatenate(_split_bf16(x), axis=0)
    stacked = jnp.concatenate([parts[:, j:j + w] for j in range(0, n, w)], axis=0)
    res = _dot(stacked, mask01)
    blocks = [res[j:j + 3 * m] for j in range(0, res.shape[0], 3 * m)]
    return jnp.concatenate([b[:m] + (b[m:2 * m] + b[2 * m:]) for b in blocks], axis=1)


def _zero_after(x):
    bits = pltpu.bitcast(x[0:8, 0:128], jnp.uint32)
    sixteen = jnp.uint32(16)
    return lax.shift_right_logical(lax.shift_right_logical(bits, sixteen), sixteen).astype(F32)


def _rms(x, w):
    return x * lax.rsqrt(jnp.mean(x * x, -1, keepdims=True) + EPS) * w


def _silu(x):
    h = 0.5 * x
    return h + h * jnp.tanh(h)


def _gelu_tanh(x):
    c = 0.7978845608028654
    h = 0.5 * x
    return h + h * jnp.tanh(x * (c + (c * 0.044715) * (x * x)))


def _sigmoid_mix(ga, ya, gb, yb):
    return 0.5 * ((ya + yb) + (jnp.tanh(0.5 * ga) * ya + jnp.tanh(0.5 * gb) * yb))


def _params(*sem):
    return pltpu.CompilerParams(dimension_semantics=sem, vmem_limit_bytes=VMEM_LIMIT_BYTES)


def _const_spec(shape):
    nd = len(shape)
    return pl.BlockSpec(shape, lambda *_: (0,) * nd)


def _gate_math(ab, a_log, dt_bias, is_beta):
    z = ab + dt_bias
    softplus = jnp.maximum(z, 0.0) + jnp.log1p(jnp.exp(-jnp.abs(z)))
    return jnp.where(is_beta, jax.nn.sigmoid(ab), -jnp.exp(a_log) * softplus)


def _inproj_kernel(x0_ref, xnext_ref, xp_ref, xn_ref, nw_ref, wqkv_ref, cw_ref, wdng_ref, wsgu_ref, wsgv_ref,
                   wga_ref, wgb_ref, wabt_ref, alog_ref, dtb_ref,
                   qkv_ref, dng_ref, sgu_ref, sgv_ref, ga_ref, gb_ref, gatet_ref, raw_ref, h_ref, hnext_ref,
                   *, tiles_per_seq):
    tm = hnext_ref.shape[0]
    h0 = CONV_HALO_ROWS
    pad = CONV_K // 2
    mm_cols = 256
    pos = pl.program_id(0) % tiles_per_seq
    half = tm // 2

    @pl.when(pl.program_id(0) == 0)
    def _():
        hnext_ref[...] = _rms(x0_ref[...], nw_ref[...]).astype(BF16)

    h_ref[h0:h0 + tm, :] = hnext_ref[...]
    h_ref[0:h0, :] = _rms(xp_ref[...], nw_ref[...]).astype(BF16)
    h_ref[h0 + tm:2 * h0 + tm, :] = _rms(xn_ref[...], nw_ref[...]).astype(BF16)
    hs = {r0: h_ref[h0 + r0:h0 + r0 + half, :] for r0 in (0, half)}
    keep_prev = jnp.where(pos > 0, 1.0, 0.0)
    keep_next = jnp.where(pos < tiles_per_seq - 1, 1.0, 0.0)

    def qkv_chunk(j):
        res = _dot(h_ref[...], wqkv_ref[:, j:j + mm_cols])
        raw_ref[:, j:j + mm_cols] = res
        raw_ref[0:h0, j:j + mm_cols] = res[0:h0] * keep_prev
        raw_ref[h0 + tm:2 * h0 + tm, j:j + mm_cols] = res[h0 + tm:2 * h0 + tm] * keep_next

    def conv_unit(g, r, after):
        cols = slice(g * DN_D, (g + 1) * DN_D)
        lo = h0 - pad + r
        acc = raw_ref[lo:lo + ROW_BLOCK, cols] * (cw_ref[0:1, cols] + after[0:1, :])
        for j in range(1, CONV_K):
            acc = acc + raw_ref[lo + j:lo + j + ROW_BLOCK, cols] * cw_ref[j:j + 1, cols]
        y = _silu(acc)
        if g < 2 * DN_HEADS:
            y = y * lax.rsqrt(jnp.sum(y * y, -1, keepdims=True) + EPS)
        if g < DN_HEADS:
            y = y * (DN_D ** -0.5)
        qkv_ref[r:r + ROW_BLOCK, cols] = y.astype(BF16)

    rest = [(w_ref, o_ref, j, r0)
            for w_ref, o_ref in ((wdng_ref, dng_ref), (wsgu_ref, sgu_ref), (wsgv_ref, sgv_ref),
                                 (wga_ref, ga_ref), (wgb_ref, gb_ref))
            for j in range(0, w_ref.shape[1], mm_cols) for r0 in (0, half)]
    n_rest = len(rest)

    def rest_unit(w_ref, o_ref, j, r0):
        res = _dot(hs[r0], w_ref[:, j:j + mm_cols])
        o_ref[r0:r0 + half, j:j + mm_cols] = res.astype(o_ref.dtype)
        return _zero_after(res)

    groups_per_chunk = mm_cols // DN_D
    n_chunks = 3 * DN_W // mm_cols
    n_conv = 3 * DN_HEADS * (tm // ROW_BLOCK)
    done = 0
    after = jnp.zeros((8, DN_D), F32)
    norm_rows = ROW_BLOCK // 2
    norm_units = list(range(0, tm, norm_rows))
    n_norm = len(norm_units)

    def norm_unit(r, after):
        nw = nw_ref[...] + jnp.concatenate([after[0:1, :]] * (nw_ref.shape[1] // DN_D), axis=1)
        hnext_ref[r:r + norm_rows, :] = _rms(xnext_ref[r:r + norm_rows, :], nw).astype(BF16)

    qkv_chunk(0)
    for ch in range(n_chunks):
        if ch + 1 < n_chunks:
            qkv_chunk((ch + 1) * mm_cols)
        for g in range(ch * groups_per_chunk, (ch + 1) * groups_per_chunk):
            for r in range(0, tm, ROW_BLOCK):
                conv_unit(g, r, after)
                done += 1
                while rest and (n_rest - len(rest)) * n_conv < done * n_rest:
                    after = rest_unit(*rest.pop(0))
                while norm_units and (n_norm - len(norm_units)) * n_conv < done * n_norm:
                    norm_unit(norm_units.pop(0), after)
    while rest:
        rest_unit(*rest.pop(0))
    abt = _dot_nt(wabt_ref[...], h_ref[h0:h0 + tm, :])
    row = lax.broadcasted_iota(jnp.int32, abt.shape, 0)
    gatet_ref[...] = _gate_math(abt, alog_ref[...], dtb_ref[...], row < 2 * DN_HEADS)


def _inproj(x2d, seq, norm_w, w_in, conv_w, a_log, dt_bias, tm):
    t, d = x2d.shape
    hb = tm // CONV_HALO_ROWS
    last_halo = t // CONV_HALO_ROWS - 1
    n_qkv = 3 * DN_W
    n_ab = 4 * DN_HEADS
    bounds = [0, n_qkv, n_qkv + n_ab]
    for width in (DN_W, SG_W, SG_W, d, d):
        bounds.append(bounds[-1] + width)
    assert bounds[-1] == w_in.shape[1]
    piece = lambda k: w_in[:, bounds[k]:bounds[k + 1]].astype(BF16)
    w_qkv, w_ab = piece(0), piece(1)
    w_rest = [piece(k) for k in range(2, 7)]
    zeros8 = jnp.zeros((2 * DN_HEADS,), F32)
    alog16 = jnp.concatenate([zeros8, a_log.reshape(-1).astype(F32)])
    dtb16 = jnp.concatenate([zeros8, dt_bias.reshape(-1).astype(F32)])
    widths = [n_qkv, DN_W, SG_W, SG_W, d, d]
    out_shape = [jax.ShapeDtypeStruct((t, n), BF16) for n in widths]
    out_shape += [jax.ShapeDtypeStruct((n_ab, t), F32)]
    out_specs = [pl.BlockSpec((tm, n), lambda i: (i, 0)) for n in widths]
    out_specs += [pl.BlockSpec((n_ab, tm), lambda i: (0, i))]
    consts = [norm_w.reshape(1, d).astype(F32), w_qkv, conv_w.astype(F32), *w_rest, w_ab.T,
              alog16.reshape(n_ab, 1), dtb16.reshape(n_ab, 1)]
    n_tiles = t // tm
    x_specs = [pl.BlockSpec((tm, d), lambda i: (0, 0)),
               pl.BlockSpec((tm, d), lambda i: (jnp.minimum(i + 1, n_tiles - 1), 0)),
               pl.BlockSpec((CONV_HALO_ROWS, d), lambda i: (jnp.maximum(i * hb - 1, 0), 0)),
               pl.BlockSpec((CONV_HALO_ROWS, d), lambda i: (jnp.minimum((i + 1) * hb, last_halo), 0))]
    return pl.pallas_call(
        functools.partial(_inproj_kernel, tiles_per_seq=seq // tm),
        grid=(n_tiles,),
        in_specs=x_specs + [_const_spec(c.shape) for c in consts],
        out_specs=out_specs,
        out_shape=out_shape,
        scratch_shapes=[pltpu.VMEM((tm + 2 * CONV_HALO_ROWS, n_qkv), F32),
                        pltpu.VMEM((tm + 2 * CONV_HALO_ROWS, d), BF16), pltpu.VMEM((tm, d), BF16)],
        compiler_params=_params("arbitrary"),
        name="inproj",
    )(x2d, x2d, x2d, x2d, *consts)


def _delta_kernel(qf_ref, gtf_ref, qb_ref, gtb_ref, of_ref, ob_ref, state_ref, *, cb):
    c = DN_CHUNK
    nc = cb // c
    qkv_refs = (qf_ref, qb_ref)
    n_gate = gtf_ref.shape[0]

    @pl.when(pl.program_id(1) == 0)
    def _():
        state_ref[...] = jnp.zeros_like(state_ref)

    ri = lax.broadcasted_iota(jnp.int32, (DN_D, DN_D), 0)
    ci = lax.broadcasted_iota(jnp.int32, (DN_D, DN_D), 1)
    same = (ri & -c) == (ci & -c)
    ge = jnp.where(same & (ri >= ci), 1.0, 0.0)
    le = jnp.where(same & (ri <= ci), 1.0, 0.0)
    gates_t = (gtf_ref[...], gtb_ref[...])
    gc_t = (_cumsum_lanes(gates_t[0], le) * LOG2_E, _cumsum_lanes(gates_t[1], ge) * LOG2_E)
    is_beta = lax.broadcasted_iota(jnp.int32, (n_gate, cb), 0) < 2 * DN_HEADS
    pad_rows = jnp.zeros((DN_D - n_gate, cb), F32)
    gc = tuple(jnp.concatenate([jnp.where(is_beta, gates_t[d], gc_t[d]), pad_rows], axis=0).T for d in range(2))
    gates = gc

    ri2 = lax.broadcasted_iota(jnp.int32, (c, 2 * c), 0)
    lane2 = lax.broadcasted_iota(jnp.int32, (c, 2 * c), 1)
    ci2 = lane2 & (c - 1)
    left = lane2 < c
    eye2 = jnp.where(ri2 == ci2, 1.0, 0.0)
    ri1 = lax.broadcasted_iota(jnp.int32, (c, c), 0)
    ci1 = lax.broadcasted_iota(jnp.int32, (c, c), 1)
    incl = (ri1 >= ci1, ri1 <= ci1)
    strict = (ri1 > ci1, ri1 < ci1)
    zeros1 = jnp.zeros((c, c), F32)

    state = {(d, h): state_ref[d, h] for d in range(2) for h in range(DN_HEADS)}
    steps_done = {0: 0, 1: 0}
    o_refs = (of_ref, ob_ref)

    def group(s, dirs):
        items = [(d, (nc - 1 - s) if d else s, h) for d in dirs for h in range(DN_HEADS)]
        beta, g_col, g_last, decay = {}, {}, {}, {}

        class _Operand:
            def __init__(self, col0):
                self.col0 = col0

            def __getitem__(self, it):
                d, n, h = it
                return qkv_refs[d][n * c:(n + 1) * c, self.col0 + h * DN_D:self.col0 + (h + 1) * DN_D]

        q, k, v = _Operand(0), _Operand(DN_W), _Operand(2 * DN_W)
        qkk = {it: _dot_nt(jnp.concatenate([q[it], k[it]], axis=0), k[it]) for it in items}
        yield
        for it in items:
            d, n, h = it
            rows = slice(n * c, (n + 1) * c)
            bcol = d * DN_HEADS + h
            gcol = (2 + d) * DN_HEADS + h
            beta[it] = jnp.broadcast_to(gates[d][rows, bcol:bcol + 1], (c, DN_D))
            g_col[it] = jnp.broadcast_to(gc[d][rows, gcol:gcol + 1], (c, DN_D))
            g_row = gc_t[d][gcol:gcol + 1, rows]
            g_last[it] = g_col[it][0:1, :] if d else g_col[it][c - 1:c, :]
            decay[it] = jnp.exp2(g_col[it][:, :c] - g_row)
        yield
        a_intra = {it: jnp.where(incl[it[0]], qkk[it][:c] * decay[it], 0.0) for it in items}
        lm = {it: jnp.where(strict[it[0]], beta[it][:, :c] * qkk[it][c:] * decay[it], 0.0) for it in items}
        lm_l = {it: jnp.concatenate([lm[it], zeros1], axis=1) for it in items}
        sq = {it: _dot(lm[it], lm_l[it]) for it in items}
        yield
        z = {it: jnp.where(left, sq[it], eye2 - pltpu.roll(lm_l[it], c, 1)) for it in items}
        eg = {it: jnp.exp2(g_col[it]) for it in items}
        uw_rhs = {it: jnp.concatenate([(v[it].astype(F32) * beta[it]).astype(BF16),
                                       (k[it].astype(F32) * (beta[it] * eg[it])).astype(BF16)], axis=1)
                  for it in items}
        power = 2
        while power < c:
            zb = {it: z[it].astype(BF16) for it in items}
            r = {it: _dot(zb[it][:, :c], zb[it]) for it in items}
            yield
            keep = ~left if 2 * power < c else True
            z = {it: jnp.where(keep, z[it], 0.0) + r[it] for it in items}
            power *= 2
        uw = {it: _dot(pltpu.roll(z[it], c, 1)[:, :c], uw_rhs[it]) for it in items}
        kga = {it: jnp.concatenate([(k[it].astype(F32) * jnp.exp2(g_last[it] - g_col[it])).T.astype(BF16),
                                    a_intra[it].astype(BF16)], axis=0) for it in items}
        yield
        assert all(steps_done[d] == s for d in dirs), "recurrence steps must be issued in order"
        wq = {it: jnp.concatenate([uw[it][:, DN_D:], q[it].astype(F32) * eg[it]], axis=0) for it in items}
        ws_qs = {it: _dot(wq[it], state[it[0], it[2]]) for it in items}
        yield
        v_new = {it: uw[it][:, :DN_D] - ws_qs[it][:c] for it in items}
        upd = {it: _dot(kga[it], v_new[it]) for it in items}
        yield
        for it in items:
            d, n, h = it
            state[d, h] = state[d, h] * jnp.exp2(g_last[it]) + upd[it][:DN_D]
            o = ws_qs[it][c:] + upd[it][DN_D:]
            o_refs[d][n * c:(n + 1) * c, h * DN_D:(h + 1) * DN_D] = o.astype(BF16)
        for d in dirs:
            steps_done[d] += 1

    waiting = [group(s, dirs) for s in range(nc) for dirs in DELTA_GROUP_DIRS]
    live = []
    tick = 0
    while waiting or live:
        if waiting and tick % DELTA_GROUP_STAGGER == 0:
            live.append(waiting.pop(0))
        for g in list(live):
            try:
                next(g)
            except StopIteration:
                live.remove(g)
        tick += 1
    for (d, h), val in state.items():
        state_ref[d, h] = val


def _delta_scan(qkv, gates_t, cb):
    bsz, seq, n_qkv = qkv.shape
    nb = seq // cb
    n_gate = gates_t.shape[0]

    def stream_specs(blk):
        return [
            pl.BlockSpec((None, cb, n_qkv), lambda b, i: (b, blk(i), 0)),
            pl.BlockSpec((n_gate, cb), lambda b, i: (0, b * nb + blk(i))),
        ]

    fwd = lambda i: i
    bwd = lambda i: nb - 1 - i
    stream = (qkv, gates_t)
    return pl.pallas_call(
        functools.partial(_delta_kernel, cb=cb),
        grid=(bsz, nb),
        in_specs=stream_specs(fwd) + stream_specs(bwd),
        out_specs=[pl.BlockSpec((None, cb, DN_W), lambda b, i: (b, fwd(i), 0)),
                   pl.BlockSpec((None, cb, DN_W), lambda b, i: (b, bwd(i), 0))],
        out_shape=[jax.ShapeDtypeStruct((bsz, seq, DN_W), BF16)] * 2,
        scratch_shapes=[pltpu.VMEM((2, DN_HEADS, DN_D, DN_D), F32)],
        compiler_params=_params("parallel", "arbitrary"),
        name="delta",
    )(*stream, *stream)


def _mix_kernel(x_ref, of_ref, ob_ref, dng_ref, sgu_ref, sgv_ref, ga_ref, gb_ref,
                dnw_ref, wua_ref, lnw_ref, lnb_ref, ws_ref, bst_ref, wub_ref, wout_ref,
                y_ref, dn_s, sg_s):
    tm = x_ref.shape[0]
    for h in range(DN_HEADS):
        cols = slice(h * DN_D, (h + 1) * DN_D)
        o = of_ref[:, cols].astype(F32) + ob_ref[:, cols].astype(F32)
        dn_s[:, cols] = (_rms(o, dnw_ref[...]) * _silu(dng_ref[:, cols].astype(F32))).astype(BF16)
    y_a = _dot(dn_s[...], wua_ref[...])
    v = _gelu_tanh(sgv_ref[...].astype(F32))
    mu = jnp.mean(v, -1, keepdims=True)
    vc = v - mu
    v = vc * lax.rsqrt(jnp.mean(vc * vc, -1, keepdims=True) + EPS) * lnw_ref[...] + lnb_ref[...]
    v = v.astype(BF16)
    for n in range(tm // SG_CHUNK):
        rows = slice(n * SG_CHUNK, (n + 1) * SG_CHUNK)
        for g in range(SG_GROUPS):
            cols = slice(g * SG_CHUNK, (g + 1) * SG_CHUNK)
            mixed = _dot(ws_ref[g], v[rows, cols]) + bst_ref[:, g:g + 1]
            sg_s[rows, cols] = (_gelu_tanh(sgu_ref[rows, cols].astype(F32)) * mixed).astype(BF16)
    y_b = _dot(sg_s[...], wub_ref[...])
    merged = _sigmoid_mix(ga_ref[...].astype(F32), y_a, gb_ref[...].astype(F32), y_b)
    y_ref[...] = x_ref[...] + _dot(merged, wout_ref[...])


def _mix(x2d, o_f, o_b, dng, sgu, sgv, ga, gb, dn_norm_w, w_up_a, ln_w, ln_b, sg_w, sg_b, w_up_b, w_out, tm):
    t, d = x2d.shape
    consts = [dn_norm_w.reshape(1, DN_D).astype(F32), w_up_a.astype(BF16),
              ln_w.reshape(1, SG_W).astype(F32), ln_b.reshape(1, SG_W).astype(F32),
              sg_w.astype(BF16), sg_b.T.astype(F32), w_up_b.astype(BF16), w_out.astype(BF16)]
    acts = [x2d, o_f, o_b, dng, sgu, sgv, ga, gb]
    return pl.pallas_call(
        _mix_kernel,
        grid=(t // tm,),
        in_specs=[pl.BlockSpec((tm, a.shape[1]), lambda i: (i, 0)) for a in acts]
                 + [_const_spec(c.shape) for c in consts],
        out_specs=pl.BlockSpec((tm, d), lambda i: (i, 0)),
        out_shape=jax.ShapeDtypeStruct((t, d), F32),
        scratch_shapes=[pltpu.VMEM((tm, DN_W), BF16), pltpu.VMEM((tm, SG_W), BF16)],
        compiler_params=_params("parallel"),
        name="mix",
    )(*acts, *consts)


def _norm_proj_kernel(x_ref, nw_ref, w_ref, o_ref):
    o_ref[...] = _dot(_rms(x_ref[...], nw_ref[...]), w_ref[...]).astype(o_ref.dtype)


def _norm_proj(x2d, norm_w, w, tm):
    t, d = x2d.shape
    n = w.shape[1]
    return pl.pallas_call(
        _norm_proj_kernel,
        grid=(t // tm,),
        in_specs=[pl.BlockSpec((tm, d), lambda i: (i, 0)), _const_spec((1, d)), _const_spec(w.shape)],
        out_specs=pl.BlockSpec((tm, n), lambda i: (i, 0)),
        out_shape=jax.ShapeDtypeStruct((t, n), BF16),
        compiler_params=_params("parallel"),
        name="mem_kv",
    )(x2d, norm_w.reshape(1, d).astype(F32), w.astype(BF16))


def _xattn_kernel(x_ref, kv_ref, nw_ref, wq_ref, wo_ref, y_ref, o_s):
    d = x_ref.shape[1]
    hd = d // XA_HEADS
    x = x_ref[...]
    q = _dot(_rms(x, nw_ref[...]), wq_ref[...])
    for h in range(XA_HEADS):
        cols = slice(h * hd, (h + 1) * hd)
        s = _dot_nt(q[:, cols], kv_ref[:, cols]) * (hd ** -0.5)
        p = jnp.exp(s - jnp.max(s, -1, keepdims=True))
        p = p / jnp.sum(p, -1, keepdims=True)
        o_s[:, cols] = _dot(p, kv_ref[:, d + h * hd:d + (h + 1) * hd]).astype(BF16)
    y_ref[...] = x + _dot(o_s[...], wo_ref[...])


def _xattn(x3d, kv, norm_w, w_q, w_o, tm):
    bsz, seq, d = x3d.shape
    n_mem = kv.shape[1]
    consts = [norm_w.reshape(1, d).astype(F32), w_q.astype(BF16), w_o.astype(BF16)]
    return pl.pallas_call(
        _xattn_kernel,
        grid=(bsz, seq // tm),
        in_specs=[pl.BlockSpec((None, tm, d), lambda b, i: (b, i, 0)),
                  pl.BlockSpec((None, n_mem, 2 * d), lambda b, i: (b, 0, 0))]
                 + [_const_spec(c.shape) for c in consts],
        out_specs=pl.BlockSpec((None, tm, d), lambda b, i: (b, i, 0)),
        out_shape=jax.ShapeDtypeStruct((bsz, seq, d), F32),
        scratch_shapes=[pltpu.VMEM((tm, d), BF16)],
        compiler_params=_params("parallel", "parallel"),
        name="xattn",
    )(x3d, kv, *consts)


def _run_staggered(gens, stagger):
    waiting = list(gens)
    live = []
    tick = 0
    while waiting or live:
        if waiting and tick % stagger == 0:
            live.append(waiting.pop(0))
        for g in list(live):
            try:
                next(g)
            except StopIteration:
                live.remove(g)
        tick += 1


def _mixattn_kernel(x_ref, of_ref, ob_ref, dng_ref, sgu_ref, sgv_ref, ga_ref, gb_ref, kv_ref,
                    dnw_ref, wua_ref, lnw_ref, lnb_ref, ws_ref, bst_ref, wub_ref, wout_ref,
                    xnw_ref, wq_ref, wo_ref, y_ref):
    tm, d = x_ref.shape
    hd = d // XA_HEADS
    half = tm // 2
    pace = [jnp.zeros((8, DN_D), F32)]

    def paced_row(ref):
        n = ref.shape[1] // DN_D
        return ref[...] + jnp.concatenate([pace[0][0:1, :]] * n, axis=1)

    def sub_tile(r0):
        rows = slice(r0, r0 + half)
        dnw = paced_row(dnw_ref)
        dn = []
        for h in range(DN_HEADS):
            cols = slice(h * DN_D, (h + 1) * DN_D)
            o = of_ref[rows, cols].astype(F32) + ob_ref[rows, cols].astype(F32)
            dn.append((_rms(o, dnw) * _silu(dng_ref[rows, cols].astype(F32))).astype(BF16))
        yield
        lnb = paced_row(lnb_ref)
        y_a = _dot(jnp.concatenate(dn, axis=1), wua_ref[...])
        pace[0] = _zero_after(y_a)
        v = _gelu_tanh(sgv_ref[rows, :].astype(F32))
        vc = v - jnp.mean(v, -1, keepdims=True)
        v = (vc * lax.rsqrt(jnp.mean(vc * vc, -1, keepdims=True) + EPS) * lnw_ref[...] + lnb).astype(BF16)
        yield
        sg = []
        for n in range(half // SG_CHUNK):
            blk = slice(r0 + n * SG_CHUNK, r0 + (n + 1) * SG_CHUNK)
            sg_n = []
            for g in range(SG_GROUPS):
                cols = slice(g * SG_CHUNK, (g + 1) * SG_CHUNK)
                mixed = _dot(ws_ref[g], v[n * SG_CHUNK:(n + 1) * SG_CHUNK, cols]) + bst_ref[:, g:g + 1]
                sg_n.append((_gelu_tanh(sgu_ref[blk, cols].astype(F32)) * mixed).astype(BF16))
            sg.append(jnp.concatenate(sg_n, axis=1))
        yield
        y_b = _dot(jnp.concatenate(sg, axis=0), wub_ref[...])
        merged = _sigmoid_mix(ga_ref[rows, :].astype(F32), y_a, gb_ref[rows, :].astype(F32), y_b)
        yield
        x1 = x_ref[rows, :] + _dot(merged, wout_ref[...])
        pace[0] = _zero_after(x1)
        yield
        q = _dot(_rms(x1, xnw_ref[...]), wq_ref[...])
        pace[0] = _zero_after(q)
        yield
        o = []
        for h in range(XA_HEADS):
            cols = slice(h * hd, (h + 1) * hd)
            s = _dot_nt(q[:, cols], kv_ref[:, cols]) * (hd ** -0.5)
            p = jnp.exp(s - jnp.max(s, -1, keepdims=True))
            p = p / jnp.sum(p, -1, keepdims=True)
            o.append(_dot(p, kv_ref[:, d + h * hd:d + (h + 1) * hd]).astype(BF16))
        yield
        y = x1 + _dot(jnp.concatenate(o, axis=1), wo_ref[...])
        pace[0] = _zero_after(y)
        y_ref[rows, :] = y

    _run_staggered([sub_tile(0), sub_tile(half)], MIXATTN_STAGGER)


def _mixattn(x3d, o_f, o_b, dng, sgu, sgv, ga, gb, kv, dn_norm_w, w_up_a, ln_w, ln_b, sg_w, sg_b, w_up_b, w_out,
             xa_norm_w, w_q, w_o, tm):
    bsz, seq, d = x3d.shape
    n_mem = kv.shape[1]
    consts = [dn_norm_w.reshape(1, DN_D).astype(F32), w_up_a.astype(BF16),
              ln_w.reshape(1, SG_W).astype(F32), ln_b.reshape(1, SG_W).astype(F32),
              sg_w.astype(BF16), sg_b.T.astype(F32), w_up_b.astype(BF16), w_out.astype(BF16),
              xa_norm_w.reshape(1, d).astype(F32), w_q.astype(BF16), w_o.astype(BF16)]
    acts = [x3d] + [a.reshape(bsz, seq, -1) for a in (o_f, o_b, dng, sgu, sgv, ga, gb)]
    return pl.pallas_call(
        _mixattn_kernel,
        grid=(bsz, seq // tm),
        in_specs=[pl.BlockSpec((None, tm, a.shape[2]), lambda b, i: (b, i, 0)) for a in acts]
                 + [pl.BlockSpec((None, n_mem, 2 * d), lambda b, i: (b, 0, 0))]
                 + [_const_spec(c.shape) for c in consts],
        out_specs=pl.BlockSpec((None, tm, d), lambda b, i: (b, i, 0)),
        out_shape=jax.ShapeDtypeStruct((bsz, seq, d), F32),
        compiler_params=_params("parallel", "parallel"),
        name="mixattn",
    )(*acts, kv, *consts)


def _ffn_kernel(x_ref, nw_ref, wg_ref, wu_ref, wd_ref, fw_ref, y_ref, *, ff_chunks):
    x = x_ref[...]
    h = _rms(x, nw_ref[...]).astype(BF16)
    acc = x
    for lo, hi in ff_chunks:
        a = _silu(_dot(h, wg_ref[:, lo:hi])) * _dot(h, wu_ref[:, lo:hi])
        acc = acc + _dot(a, wd_ref[lo:hi, :])
    y_ref[...] = _rms(acc, fw_ref[...])


def _ffn(x2d, norm_w, w_gate_up, w_down, final_w, tm):
    t, d = x2d.shape
    d_ff = w_down.shape[0]
    mxu_cols = 256
    assert d_ff % mxu_cols == 0
    half = (d_ff // mxu_cols + 1) // 2 * mxu_cols
    ff_chunks = ((0, half), (half, d_ff))
    consts = [norm_w.reshape(1, d).astype(F32), w_gate_up[:, :d_ff].astype(BF16),
              w_gate_up[:, d_ff:].astype(BF16), w_down.astype(BF16), final_w.reshape(1, d).astype(F32)]
    return pl.pallas_call(
        functools.partial(_ffn_kernel, ff_chunks=ff_chunks),
        grid=(t // tm,),
        in_specs=[pl.BlockSpec((tm, d), lambda i: (i, 0))] + [_const_spec(c.shape) for c in consts],
        out_specs=pl.BlockSpec((tm, d), lambda i: (i, 0)),
        out_shape=jax.ShapeDtypeStruct((t, d), F32),
        compiler_params=_params("parallel"),
        name="ffn",
    )(x2d, *consts)


def _post_kernel(x_ref, of_ref, ob_ref, dng_ref, sgu_ref, sgv_ref, ga_ref, gb_ref, kv_ref,
                 dnw_ref, wua_ref, lnw_ref, lnb_ref, ws_ref, bst_ref, wub_ref, wout_ref,
                 xnw_ref, wq_ref, wo_ref, fnw_ref, wg_ref, wu_ref, wd_ref, fw_ref,
                 y_ref, x2_s, *, ff_chunks):
    tm, d = x_ref.shape
    hd = d // XA_HEADS
    i = pl.program_id(0)
    slot = i % 2

    @pl.when(i == 0)
    def _():
        x2_s[...] = jnp.zeros_like(x2_s)

    pace = [jnp.zeros((8, DN_D), F32)]

    def paced_row(ref):
        n = ref.shape[1] // DN_D
        return ref[...] + jnp.concatenate([pace[0][0:1, :]] * n, axis=1)

    def ffn_stream():
        x2 = x2_s[1 - slot]
        h = _rms(x2, fnw_ref[...]).astype(BF16)
        acc = x2
        yield
        for lo, hi in ff_chunks:
            a = _silu(_dot(h, wg_ref[:, lo:hi])) * _dot(h, wu_ref[:, lo:hi])
            acc = acc + _dot(a, wd_ref[lo:hi, :])
            pace[0] = _zero_after(acc)
            yield
        y_ref[...] = _rms(acc, fw_ref[...])

    def mix_stream():
        dnw = paced_row(dnw_ref)
        dn = []
        for h in range(DN_HEADS):
            cols = slice(h * DN_D, (h + 1) * DN_D)
            o = of_ref[:, cols].astype(F32) + ob_ref[:, cols].astype(F32)
            dn.append((_rms(o, dnw) * _silu(dng_ref[:, cols].astype(F32))).astype(BF16))
        yield
        lnb = paced_row(lnb_ref)
        y_a = _dot(jnp.concatenate(dn, axis=1), wua_ref[...])
        v = _gelu_tanh(sgv_ref[...].astype(F32))
        vc = v - jnp.mean(v, -1, keepdims=True)
        v = (vc * lax.rsqrt(jnp.mean(vc * vc, -1, keepdims=True) + EPS) * lnw_ref[...] + lnb).astype(BF16)
        yield
        bst = bst_ref[...]
        sg = []
        for n in range(tm // SG_CHUNK):
            blk = slice(n * SG_CHUNK, (n + 1) * SG_CHUNK)
            sg_n = []
            for g in range(SG_GROUPS):
                cols = slice(g * SG_CHUNK, (g + 1) * SG_CHUNK)
                mixed = _dot(ws_ref[g], v[blk, cols]) + bst[:, g:g + 1]
                sg_n.append((_gelu_tanh(sgu_ref[blk, cols].astype(F32)) * mixed).astype(BF16))
            sg.append(jnp.concatenate(sg_n, axis=1))
        yield
        y_b = _dot(jnp.concatenate(sg, axis=0), wub_ref[...])
        merged = _sigmoid_mix(ga_ref[...].astype(F32), y_a, gb_ref[...].astype(F32), y_b)
        yield
        x1 = x_ref[...] + _dot(merged, wout_ref[...])
        yield
        q = _dot(_rms(x1, xnw_ref[...]), wq_ref[...])
        yield
        o = []
        for h in range(XA_HEADS):
            cols = slice(h * hd, (h + 1) * hd)
            s = _dot_nt(q[:, cols], kv_ref[:, cols]) * (hd ** -0.5)
            p = jnp.exp(s - jnp.max(s, -1, keepdims=True))
            p = p / jnp.sum(p, -1, keepdims=True)
            o.append(_dot(p, kv_ref[:, d + h * hd:d + (h + 1) * hd]).astype(BF16))
        yield
        x2_s[slot] = x1 + _dot(jnp.concatenate(o, axis=1), wo_ref[...])

    _run_staggered([ffn_stream(), mix_stream()], 1)


def _post(x2d, seq, o_f, o_b, dng, sgu, sgv, ga, gb, kv, dn_norm_w, w_up_a, ln_w, ln_b, sg_w, sg_b, w_up_b, w_out,
          xa_norm_w, w_q, w_o, ffn_norm_w, w_gate_up, w_down, final_w, tm):
    t, d = x2d.shape
    n_tiles = t // tm
    tiles_per_seq = seq // tm
    n_mem = kv.shape[1]
    d_ff = w_down.shape[0]
    ff_step = 512
    ff_chunks = tuple((lo, min(lo + ff_step, d_ff)) for lo in range(0, d_ff, ff_step))
    consts = [dn_norm_w.reshape(1, DN_D).astype(F32), w_up_a.astype(BF16),
              ln_w.reshape(1, SG_W).astype(F32), ln_b.reshape(1, SG_W).astype(F32),
              sg_w.astype(BF16), sg_b.T.astype(F32), w_up_b.astype(BF16), w_out.astype(BF16),
              xa_norm_w.reshape(1, d).astype(F32), w_q.astype(BF16), w_o.astype(BF16),
              ffn_norm_w.reshape(1, d).astype(F32), w_gate_up[:, :d_ff].astype(BF16),
              w_gate_up[:, d_ff:].astype(BF16), w_down.astype(BF16), final_w.reshape(1, d).astype(F32)]
    acts = [x2d] + [a.reshape(t, -1) for a in (o_f, o_b, dng, sgu, sgv, ga, gb)]
    cur = lambda i: jnp.minimum(i, n_tiles - 1)
    prev = lambda i: jnp.maximum(i - 1, 0)
    return pl.pallas_call(
        functools.partial(_post_kernel, ff_chunks=ff_chunks),
        grid=(n_tiles + 1,),
        in_specs=[pl.BlockSpec((tm, a.shape[1]), lambda i: (cur(i), 0)) for a in acts]
                 + [pl.BlockSpec((None, n_mem, 2 * d), lambda i: (cur(i) // tiles_per_seq, 0, 0))]
                 + [_const_spec(c.shape) for c in consts],
        out_specs=pl.BlockSpec((tm, d), lambda i: (prev(i), 0)),
        out_shape=jax.ShapeDtypeStruct((t, d), F32),
        scratch_shapes=[pltpu.VMEM((2, tm, d), F32)],
        compiler_params=_params("arbitrary"),
        name="post",
    )(*acts, kv, *consts)


def _largest_tile(n, cap):
    t = cap
    while n % t:
        t //= 2
    return t


def _trunk(x, mem, norm_mix_w, w_in, conv_w, dn_a_log, dn_dt_bias, dn_norm_w, w_up_a, sg_ln_w, sg_ln_b,
           sg_w, sg_b, w_up_b, w_out, norm_xa_w, norm_mem_w, xa_w_q, xa_w_kv, xa_w_o, norm_ffn_w,
           ffn_w_gate_up, ffn_w_down, final_norm_w):
    bsz, seq, d = x.shape
    t = bsz * seq
    assert seq % SG_CHUNK == 0 and seq % DN_CHUNK == 0
    tm = _largest_tile(seq, 512)
    cb = _largest_tile(seq, 512)
    x2d = x.reshape(t, d)
    qkv, dng, sgu, sgv, ga, gb, gates_t = _inproj(x2d, seq, norm_mix_w, w_in, conv_w, dn_a_log, dn_dt_bias, tm)
    o_f, o_b = _delta_scan(qkv.reshape(bsz, seq, -1), gates_t, cb)
    n_mem = mem.shape[1]
    kv = _norm_proj(mem.reshape(bsz * n_mem, d), norm_mem_w, xa_w_kv, _largest_tile(bsz * n_mem, 256))
    x2d = _mix(x2d, o_f.reshape(t, DN_W), o_b.reshape(t, DN_W), dng, sgu, sgv, ga, gb, dn_norm_w, w_up_a,
               sg_ln_w, sg_ln_b, sg_w, sg_b, w_up_b, w_out, tm)
    x3d = _xattn(x2d.reshape(bsz, seq, d), kv.reshape(bsz, n_mem, 2 * d), norm_xa_w, xa_w_q, xa_w_o, tm)
    y = _ffn(x3d.reshape(t, d), norm_ffn_w, ffn_w_gate_up, ffn_w_down, final_norm_w, tm)
    return y.reshape(bsz, seq, d)


def kernel(x_prompt, x_sample, mem_prompt, mem_sample, norm_mix_w, w_in, conv_w, dn_a_log, dn_dt_bias, dn_norm_w, w_up_a, sg_ln_w, sg_ln_b, sg_w, sg_b, w_up_b, w_out, norm_xa_w, norm_mem_w, xa_w_q, xa_w_kv, xa_w_o, norm_ffn_w, ffn_w_gate_up, ffn_w_down, final_norm_w):
    depth = w_in.shape[0]
    assert depth == 1, "the FFN kernel fuses the final norm, which assumes a single layer"
    layer = (norm_mix_w[0], w_in[0], conv_w[0], dn_a_log[0], dn_dt_bias[0], dn_norm_w[0], w_up_a[0],
             sg_ln_w[0], sg_ln_b[0], sg_w[0], sg_b[0], w_up_b[0], w_out[0], norm_xa_w[0], norm_mem_w[0],
             xa_w_q[0], xa_w_kv[0], xa_w_o[0], norm_ffn_w[0], ffn_w_gate_up[0], ffn_w_down[0], final_norm_w)
    y_prompt = _trunk(x_prompt, mem_prompt, *layer)
    y_sample = _trunk(x_sample, mem_sample, *layer)
    return (y_prompt, y_sample)
```

```python
import functools

import jax
import jax.numpy as jnp
from jax import lax
from jax.experimental import pallas as pl
from jax.experimental.pallas import tpu as pltpu

F32 = jnp.float32
BF16 = jnp.bfloat16
EPS = 1e-6
LOG2_E = 1.4426950408889634

DN_HEADS = 4
DN_D = 128
DN_W = DN_HEADS * DN_D
DN_CHUNK = 64
CONV_K = 5
SG_GROUPS = 4
SG_CHUNK = 128
SG_W = SG_GROUPS * SG_CHUNK
XA_HEADS = 4

LANES = 128
MXU_COLS = 256
CONV_HALO_ROWS = 16
ROW_BLOCK = 64
DELTA_GROUP_STAGGER = 2
ROW_TILE = 512
DELTA_BLOCK = 1024
VMEM_LIMIT_BYTES = 56 * 1024 * 1024


def _dot(a, b):
    return jnp.dot(a.astype(BF16), b.astype(BF16), preferred_element_type=F32)


def _dot_nt(a, b):
    return lax.dot_general(a.astype(BF16), b.astype(BF16), (((1,), (1,)), ((), ())),
                           preferred_element_type=F32)


def _split_bf16(x):
    hi = x.astype(BF16)
    r = x - hi.astype(F32)
    mid = r.astype(BF16)
    lo = (r - mid.astype(F32)).astype(BF16)
    return hi, mid, lo


def _cumsum_lanes(x, mask01):
    m, n = x.shape
    w = mask01.shape[0]
    parts = jnp.concatenate(_split_bf16(x), axis=0)
    stacked = jnp.concatenate([parts[:, j:j + w] for j in range(0, n, w)], axis=0)
    res = _dot(stacked, mask01)
    blocks = [res[j:j + 3 * m] for j in range(0, res.shape[0], 3 * m)]
    return jnp.concatenate([b[:m] + (b[m:2 * m] + b[2 * m:]) for b in blocks], axis=1)


def _zero_after(x):
    bits = pltpu.bitcast(x[0:8, 0:LANES], jnp.uint32)
    sixteen = jnp.uint32(16)
    return lax.shift_right_logical(lax.shift_right_logical(bits, sixteen), sixteen).astype(F32)


def _rms(x, w):
    return x * lax.rsqrt(jnp.mean(x * x, -1, keepdims=True) + EPS) * w


def _silu(x):
    h = 0.5 * x
    return h + h * jnp.tanh(h)


def _gelu_tanh(x):
    c = 0.7978845608028654
    h = 0.5 * x
    return h + h * jnp.tanh(x * (c + (c * 0.044715) * (x * x)))


def _sigmoid_mix(ga, ya, gb, yb):
    return 0.5 * ((ya + yb) + (jnp.tanh(0.5 * ga) * ya + jnp.tanh(0.5 * gb) * yb))


def _params(*sem):
    return pltpu.CompilerParams(dimension_semantics=sem, vmem_limit_bytes=VMEM_LIMIT_BYTES)


def _const_spec(shape):
    nd = len(shape)
    return pl.BlockSpec(shape, lambda *_: (0,) * nd)


def _run_staggered(gens, stagger):
    waiting = list(gens)
    live = []
    tick = 0
    while waiting or live:
        if waiting and tick % stagger == 0:
            live.append(waiting.pop(0))
        for g in list(live):
            try:
                next(g)
            except StopIteration:
                live.remove(g)
        tick += 1


def _gate_math(ab, a_log, dt_bias, is_beta):
    z = ab + dt_bias
    softplus = jnp.maximum(z, 0.0) + jnp.log1p(jnp.exp(-jnp.abs(z)))
    return jnp.where(is_beta, jax.nn.sigmoid(ab), -jnp.exp(a_log) * softplus)


def _inproj_kernel(x0_ref, xnext_ref, xp_ref, xn_ref, nw_ref, wqkv_ref, cw_ref, wdng_ref, wsgu_ref, wsgv_ref,
                   wga_ref, wgb_ref, wabt_ref, alog_ref, dtb_ref,
                   qkv_ref, dng_ref, sgu_ref, sgv_ref, ga_ref, gb_ref, gatet_ref, raw_ref, h_ref, hnext_ref,
                   *, tiles_per_seq):
    tm = hnext_ref.shape[0]
    h0 = CONV_HALO_ROWS
    pad = CONV_K // 2
    pos = pl.program_id(0) % tiles_per_seq
    half = tm // 2

    @pl.when(pl.program_id(0) == 0)
    def _():
        hnext_ref[...] = _rms(x0_ref[...], nw_ref[...]).astype(BF16)

    h_ref[h0:h0 + tm, :] = hnext_ref[...]
    h_ref[0:h0, :] = _rms(xp_ref[...], nw_ref[...]).astype(BF16)
    h_ref[h0 + tm:2 * h0 + tm, :] = _rms(xn_ref[...], nw_ref[...]).astype(BF16)
    keep_prev = jnp.where(pos > 0, 1.0, 0.0)
    keep_next = jnp.where(pos < tiles_per_seq - 1, 1.0, 0.0)

    def qkv_chunk(j):
        res = _dot(h_ref[...], wqkv_ref[:, j:j + MXU_COLS])
        raw_ref[:, j:j + MXU_COLS] = res
        raw_ref[0:h0, j:j + MXU_COLS] = res[0:h0] * keep_prev
        raw_ref[h0 + tm:2 * h0 + tm, j:j + MXU_COLS] = res[h0 + tm:2 * h0 + tm] * keep_next

    def conv_unit(g, r, after):
        cols = slice(g * DN_D, (g + 1) * DN_D)
        lo = h0 - pad + r
        acc = raw_ref[lo:lo + ROW_BLOCK, cols] * (cw_ref[0:1, cols] + after[0:1, :])
        for j in range(1, CONV_K):
            acc = acc + raw_ref[lo + j:lo + j + ROW_BLOCK, cols] * cw_ref[j:j + 1, cols]
        y = _silu(acc)
        if g < 2 * DN_HEADS:
            y = y * lax.rsqrt(jnp.sum(y * y, -1, keepdims=True) + EPS)
        if g < DN_HEADS:
            y = y * (DN_D ** -0.5)
        qkv_ref[r:r + ROW_BLOCK, cols] = y.astype(BF16)

    rest = [(w_ref, o_ref, j, r0)
            for w_ref, o_ref in ((wdng_ref, dng_ref), (wsgu_ref, sgu_ref), (wsgv_ref, sgv_ref),
                                 (wga_ref, ga_ref), (wgb_ref, gb_ref))
            for j in range(0, w_ref.shape[1], MXU_COLS) for r0 in (0, half)]
    n_rest = len(rest)

    def rest_unit(w_ref, o_ref, j, r0):
        res = _dot(h_ref[h0 + r0:h0 + r0 + half, :], w_ref[:, j:j + MXU_COLS])
        o_ref[r0:r0 + half, j:j + MXU_COLS] = res.astype(o_ref.dtype)
        return _zero_after(res)

    norm_rows = ROW_BLOCK // 2
    norm_units = list(range(0, tm, norm_rows))
    n_norm = len(norm_units)

    def norm_unit(r, after):
        nw = nw_ref[...] + jnp.concatenate([after[0:1, :]] * (nw_ref.shape[1] // LANES), axis=1)
        hnext_ref[r:r + norm_rows, :] = _rms(xnext_ref[r:r + norm_rows, :], nw).astype(BF16)

    groups_per_chunk = MXU_COLS // DN_D
    n_chunks = 3 * DN_W // MXU_COLS
    n_conv = 3 * DN_HEADS * (tm // ROW_BLOCK)
    done = 0
    after = jnp.zeros((8, LANES), F32)
    qkv_chunk(0)
    for ch in range(n_chunks):
        if ch + 1 < n_chunks:
            qkv_chunk((ch + 1) * MXU_COLS)
        for g in range(ch * groups_per_chunk, (ch + 1) * groups_per_chunk):
            for r in range(0, tm, ROW_BLOCK):
                conv_unit(g, r, after)
                done += 1
                while rest and (n_rest - len(rest)) * n_conv < done * n_rest:
                    after = rest_unit(*rest.pop(0))
                while norm_units and (n_norm - len(norm_units)) * n_conv < done * n_norm:
                    norm_unit(norm_units.pop(0), after)
    while rest:
        rest_unit(*rest.pop(0))
    abt = _dot_nt(wabt_ref[...], h_ref[h0:h0 + tm, :])
    row = lax.broadcasted_iota(jnp.int32, abt.shape, 0)
    gatet_ref[...] = _gate_math(abt, alog_ref[...], dtb_ref[...], row < 2 * DN_HEADS)


def _inproj(x2d, seq, norm_w, w_in, conv_w, a_log, dt_bias, tm):
    t, d = x2d.shape
    hb = tm // CONV_HALO_ROWS
    last_halo = t // CONV_HALO_ROWS - 1
    n_qkv = 3 * DN_W
    n_ab = 4 * DN_HEADS
    bounds = [0, n_qkv, n_qkv + n_ab]
    for width in (DN_W, SG_W, SG_W, d, d):
        bounds.append(bounds[-1] + width)
    assert bounds[-1] == w_in.shape[1]
    piece = lambda k: w_in[:, bounds[k]:bounds[k + 1]].astype(BF16)
    w_qkv, w_ab = piece(0), piece(1)
    w_rest = [piece(k) for k in range(2, 7)]
    zeros8 = jnp.zeros((2 * DN_HEADS,), F32)
    alog16 = jnp.concatenate([zeros8, a_log.reshape(-1).astype(F32)])
    dtb16 = jnp.concatenate([zeros8, dt_bias.reshape(-1).astype(F32)])
    widths = [n_qkv, DN_W, SG_W, SG_W, d, d]
    out_shape = [jax.ShapeDtypeStruct((t, n), BF16) for n in widths]
    out_shape += [jax.ShapeDtypeStruct((n_ab, t), F32)]
    out_specs = [pl.BlockSpec((tm, n), lambda i: (i, 0)) for n in widths]
    out_specs += [pl.BlockSpec((n_ab, tm), lambda i: (0, i))]
    consts = [norm_w.reshape(1, d).astype(F32), w_qkv, conv_w.astype(F32), *w_rest, w_ab.T,
              alog16.reshape(n_ab, 1), dtb16.reshape(n_ab, 1)]
    n_tiles = t // tm
    x_specs = [pl.BlockSpec((tm, d), lambda i: (0, 0)),
               pl.BlockSpec((tm, d), lambda i: (jnp.minimum(i + 1, n_tiles - 1), 0)),
               pl.BlockSpec((CONV_HALO_ROWS, d), lambda i: (jnp.maximum(i * hb - 1, 0), 0)),
               pl.BlockSpec((CONV_HALO_ROWS, d), lambda i: (jnp.minimum((i + 1) * hb, last_halo), 0))]
    return pl.pallas_call(
        functools.partial(_inproj_kernel, tiles_per_seq=seq // tm),
        grid=(n_tiles,),
        in_specs=x_specs + [_const_spec(c.shape) for c in consts],
        out_specs=out_specs,
        out_shape=out_shape,
        scratch_shapes=[pltpu.VMEM((tm + 2 * CONV_HALO_ROWS, n_qkv), F32),
                        pltpu.VMEM((tm + 2 * CONV_HALO_ROWS, d), BF16), pltpu.VMEM((tm, d), BF16)],
        compiler_params=_params("arbitrary"),
        name="inproj",
    )(x2d, x2d, x2d, x2d, *consts)


def _delta_kernel(qf_ref, gtf_ref, qb_ref, gtb_ref, of_ref, ob_ref, state_ref, *, cb):
    c = DN_CHUNK
    nc = cb // c
    qkv_refs = (qf_ref, qb_ref)
    n_gate = gtf_ref.shape[0]

    @pl.when(pl.program_id(1) == 0)
    def _():
        state_ref[...] = jnp.zeros_like(state_ref)

    ri = lax.broadcasted_iota(jnp.int32, (LANES, LANES), 0)
    ci = lax.broadcasted_iota(jnp.int32, (LANES, LANES), 1)
    same = (ri & -c) == (ci & -c)
    ge = jnp.where(same & (ri >= ci), 1.0, 0.0)
    le = jnp.where(same & (ri <= ci), 1.0, 0.0)
    gates_t = (gtf_ref[...], gtb_ref[...])
    gc_t = (_cumsum_lanes(gates_t[0], le) * LOG2_E, _cumsum_lanes(gates_t[1], ge) * LOG2_E)
    is_beta = lax.broadcasted_iota(jnp.int32, (n_gate, cb), 0) < 2 * DN_HEADS
    pad_rows = jnp.zeros((LANES - n_gate, cb), F32)
    gc = tuple(jnp.concatenate([jnp.where(is_beta, gates_t[d], gc_t[d]), pad_rows], axis=0).T for d in range(2))

    ri2 = lax.broadcasted_iota(jnp.int32, (c, 2 * c), 0)
    lane2 = lax.broadcasted_iota(jnp.int32, (c, 2 * c), 1)
    left = lane2 < c
    eye2 = jnp.where(ri2 == (lane2 & (c - 1)), 1.0, 0.0)
    ri1 = lax.broadcasted_iota(jnp.int32, (c, c), 0)
    ci1 = lax.broadcasted_iota(jnp.int32, (c, c), 1)
    incl = (ri1 >= ci1, ri1 <= ci1)
    strict = (ri1 > ci1, ri1 < ci1)
    zeros1 = jnp.zeros((c, c), F32)

    state = {(d, h): state_ref[d, h] for d in range(2) for h in range(DN_HEADS)}
    steps_done = [0]
    o_refs = (of_ref, ob_ref)

    def operand(col0, it):
        d, n, h = it
        return qkv_refs[d][n * c:(n + 1) * c, col0 + h * DN_D:col0 + (h + 1) * DN_D]

    def group(s):
        items = [(d, (nc - 1 - s) if d else s, h) for d in range(2) for h in range(DN_HEADS)]
        q = functools.partial(operand, 0)
        k = functools.partial(operand, DN_W)
        v = functools.partial(operand, 2 * DN_W)
        beta, g_col, g_last, decay = {}, {}, {}, {}
        qkk = {it: _dot_nt(jnp.concatenate([q(it), k(it)], axis=0), k(it)) for it in items}
        yield
        for it in items:
            d, n, h = it
            rows = slice(n * c, (n + 1) * c)
            bcol = d * DN_HEADS + h
            gcol = (2 + d) * DN_HEADS + h
            beta[it] = jnp.broadcast_to(gc[d][rows, bcol:bcol + 1], (c, DN_D))
            g_col[it] = jnp.broadcast_to(gc[d][rows, gcol:gcol + 1], (c, DN_D))
            g_row = gc_t[d][gcol:gcol + 1, rows]
            g_last[it] = g_col[it][0:1, :] if d else g_col[it][c - 1:c, :]
            decay[it] = jnp.exp2(g_col[it][:, :c] - g_row)
        yield
        a_intra = {it: jnp.where(incl[it[0]], qkk[it][:c] * decay[it], 0.0) for it in items}
        lm = {it: jnp.where(strict[it[0]], beta[it][:, :c] * qkk[it][c:] * decay[it], 0.0) for it in items}
        lm_l = {it: jnp.concatenate([lm[it], zeros1], axis=1) for it in items}
        sq = {it: _dot(lm[it], lm_l[it]) for it in items}
        yield
        z = {it: jnp.where(left, sq[it], eye2 - pltpu.roll(lm_l[it], c, 1)) for it in items}
        eg = {it: jnp.exp2(g_col[it]) for it in items}
        uw_rhs = {it: jnp.concatenate([(v(it).astype(F32) * beta[it]).astype(BF16),
                                       (k(it).astype(F32) * (beta[it] * eg[it])).astype(BF16)], axis=1)
                  for it in items}
        power = 2
        while power < c:
            zb = {it: z[it].astype(BF16) for it in items}
            r = {it: _dot(zb[it][:, :c], zb[it]) for it in items}
            yield
            keep = ~left if 2 * power < c else True
            z = {it: jnp.where(keep, z[it], 0.0) + r[it] for it in items}
            power *= 2
        uw = {it: _dot(pltpu.roll(z[it], c, 1)[:, :c], uw_rhs[it]) for it in items}
        kga = {it: jnp.concatenate([(k(it).astype(F32) * jnp.exp2(g_last[it] - g_col[it])).T.astype(BF16),
                                    a_intra[it].astype(BF16)], axis=0) for it in items}
        yield
        assert steps_done[0] == s, "recurrence steps must be issued in order"
        wq = {it: jnp.concatenate([uw[it][:, DN_D:], q(it).astype(F32) * eg[it]], axis=0) for it in items}
        ws_qs = {it: _dot(wq[it], state[it[0], it[2]]) for it in items}
        yield
        v_new = {it: uw[it][:, :DN_D] - ws_qs[it][:c] for it in items}
        upd = {it: _dot(kga[it], v_new[it]) for it in items}
        yield
        for it in items:
            d, n, h = it
            state[d, h] = state[d, h] * jnp.exp2(g_last[it]) + upd[it][:DN_D]
            o = ws_qs[it][c:] + upd[it][DN_D:]
            o_refs[d][n * c:(n + 1) * c, h * DN_D:(h + 1) * DN_D] = o.astype(BF16)
        steps_done[0] += 1

    _run_staggered([group(s) for s in range(nc)], DELTA_GROUP_STAGGER)
    for (d, h), val in state.items():
        state_ref[d, h] = val


def _delta_scan(qkv, gates_t, cb):
    bsz, seq, n_qkv = qkv.shape
    nb = seq // cb
    n_gate = gates_t.shape[0]

    def stream_specs(blk):
        return [
            pl.BlockSpec((None, cb, n_qkv), lambda b, i: (b, blk(i), 0)),
            pl.BlockSpec((n_gate, cb), lambda b, i: (0, b * nb + blk(i))),
        ]

    fwd = lambda i: i
    bwd = lambda i: nb - 1 - i
    stream = (qkv, gates_t)
    return pl.pallas_call(
        functools.partial(_delta_kernel, cb=cb),
        grid=(bsz, nb),
        in_specs=stream_specs(fwd) + stream_specs(bwd),
        out_specs=[pl.BlockSpec((None, cb, DN_W), lambda b, i: (b, fwd(i), 0)),
                   pl.BlockSpec((None, cb, DN_W), lambda b, i: (b, bwd(i), 0))],
        out_shape=[jax.ShapeDtypeStruct((bsz, seq, DN_W), BF16)] * 2,
        scratch_shapes=[pltpu.VMEM((2, DN_HEADS, DN_D, DN_D), F32)],
        compiler_params=_params("parallel", "arbitrary"),
        name="delta",
    )(*stream, *stream)


def _mix_kernel(x_ref, of_ref, ob_ref, dng_ref, sgu_ref, sgv_ref, ga_ref, gb_ref,
                dnw_ref, wua_ref, lnw_ref, lnb_ref, ws_ref, bst_ref, wub_ref, wout_ref,
                y_ref, dn_s, sg_s):
    tm = x_ref.shape[0]
    for h in range(DN_HEADS):
        cols = slice(h * DN_D, (h + 1) * DN_D)
        o = of_ref[:, cols].astype(F32) + ob_ref[:, cols].astype(F32)
        dn_s[:, cols] = (_rms(o, dnw_ref[...]) * _silu(dng_ref[:, cols].astype(F32))).astype(BF16)
    y_a = _dot(dn_s[...], wua_ref[...])
    v = _gelu_tanh(sgv_ref[...].astype(F32))
    mu = jnp.mean(v, -1, keepdims=True)
    vc = v - mu
    v = vc * lax.rsqrt(jnp.mean(vc * vc, -1, keepdims=True) + EPS) * lnw_ref[...] + lnb_ref[...]
    v = v.astype(BF16)
    for n in range(tm // SG_CHUNK):
        rows = slice(n * SG_CHUNK, (n + 1) * SG_CHUNK)
        for g in range(SG_GROUPS):
            cols = slice(g * SG_CHUNK, (g + 1) * SG_CHUNK)
            mixed = _dot(ws_ref[g], v[rows, cols]) + bst_ref[:, g:g + 1]
            sg_s[rows, cols] = (_gelu_tanh(sgu_ref[rows, cols].astype(F32)) * mixed).astype(BF16)
    y_b = _dot(sg_s[...], wub_ref[...])
    merged = _sigmoid_mix(ga_ref[...].astype(F32), y_a, gb_ref[...].astype(F32), y_b)
    y_ref[...] = x_ref[...] + _dot(merged, wout_ref[...])


def _mix(x2d, o_f, o_b, dng, sgu, sgv, ga, gb, dn_norm_w, w_up_a, ln_w, ln_b, sg_w, sg_b, w_up_b, w_out, tm):
    t, d = x2d.shape
    consts = [dn_norm_w.reshape(1, DN_D).astype(F32), w_up_a.astype(BF16),
              ln_w.reshape(1, SG_W).astype(F32), ln_b.reshape(1, SG_W).astype(F32),
              sg_w.astype(BF16), sg_b.T.astype(F32), w_up_b.astype(BF16), w_out.astype(BF16)]
    acts = [x2d, o_f, o_b, dng, sgu, sgv, ga, gb]
    return pl.pallas_call(
        _mix_kernel,
        grid=(t // tm,),
        in_specs=[pl.BlockSpec((tm, a.shape[1]), lambda i: (i, 0)) for a in acts]
                 + [_const_spec(c.shape) for c in consts],
        out_specs=pl.BlockSpec((tm, d), lambda i: (i, 0)),
        out_shape=jax.ShapeDtypeStruct((t, d), F32),
        scratch_shapes=[pltpu.VMEM((tm, DN_W), BF16), pltpu.VMEM((tm, SG_W), BF16)],
        compiler_params=_params("parallel"),
        name="mix",
    )(*acts, *consts)


def _norm_proj_kernel(x_ref, nw_ref, w_ref, o_ref):
    o_ref[...] = _dot(_rms(x_ref[...], nw_ref[...]), w_ref[...]).astype(o_ref.dtype)


def _norm_proj(x2d, norm_w, w, tm):
    t, d = x2d.shape
    n = w.shape[1]
    return pl.pallas_call(
        _norm_proj_kernel,
        grid=(t // tm,),
        in_specs=[pl.BlockSpec((tm, d), lambda i: (i, 0)), _const_spec((1, d)), _const_spec(w.shape)],
        out_specs=pl.BlockSpec((tm, n), lambda i: (i, 0)),
        out_shape=jax.ShapeDtypeStruct((t, n), BF16),
        compiler_params=_params("parallel"),
        name="mem_kv",
    )(x2d, norm_w.reshape(1, d).astype(F32), w.astype(BF16))


def _xattn_kernel(x_ref, kv_ref, nw_ref, wq_ref, wo_ref, y_ref, o_s):
    d = x_ref.shape[1]
    hd = d // XA_HEADS
    x = x_ref[...]
    q = _dot(_rms(x, nw_ref[...]), wq_ref[...])
    for h in range(XA_HEADS):
        cols = slice(h * hd, (h + 1) * hd)
        s = _dot_nt(q[:, cols], kv_ref[:, cols]) * (hd ** -0.5)
        p = jnp.exp(s - jnp.max(s, -1, keepdims=True))
        p = p / jnp.sum(p, -1, keepdims=True)
        o_s[:, cols] = _dot(p, kv_ref[:, d + h * hd:d + (h + 1) * hd]).astype(BF16)
    y_ref[...] = x + _dot(o_s[...], wo_ref[...])


def _xattn(x3d, kv, norm_w, w_q, w_o, tm):
    bsz, seq, d = x3d.shape
    n_mem = kv.shape[1]
    consts = [norm_w.reshape(1, d).astype(F32), w_q.astype(BF16), w_o.astype(BF16)]
    return pl.pallas_call(
        _xattn_kernel,
        grid=(bsz, seq // tm),
        in_specs=[pl.BlockSpec((None, tm, d), lambda b, i: (b, i, 0)),
                  pl.BlockSpec((None, n_mem, 2 * d), lambda b, i: (b, 0, 0))]
                 + [_const_spec(c.shape) for c in consts],
        out_specs=pl.BlockSpec((None, tm, d), lambda b, i: (b, i, 0)),
        out_shape=jax.ShapeDtypeStruct((bsz, seq, d), F32),
        scratch_shapes=[pltpu.VMEM((tm, d), BF16)],
        compiler_params=_params("parallel", "parallel"),
        name="xattn",
    )(x3d, kv, *consts)


def _ffn_kernel(x_ref, nw_ref, wg_ref, wu_ref, wd_ref, fw_ref, y_ref, *, ff_chunks):
    x = x_ref[...]
    h = _rms(x, nw_ref[...]).astype(BF16)
    acc = x
    for lo, hi in ff_chunks:
        a = _silu(_dot(h, wg_ref[:, lo:hi])) * _dot(h, wu_ref[:, lo:hi])
        acc = acc + _dot(a, wd_ref[lo:hi, :])
    y_ref[...] = _rms(acc, fw_ref[...])


def _ffn(x2d, norm_w, w_gate_up, w_down, final_w, tm):
    t, d = x2d.shape
    d_ff = w_down.shape[0]
    assert d_ff % MXU_COLS == 0
    half = (d_ff // MXU_COLS + 1) // 2 * MXU_COLS
    ff_chunks = ((0, half), (half, d_ff))
    consts = [norm_w.reshape(1, d).astype(F32), w_gate_up[:, :d_ff].astype(BF16),
              w_gate_up[:, d_ff:].astype(BF16), w_down.astype(BF16), final_w.reshape(1, d).astype(F32)]
    return pl.pallas_call(
        functools.partial(_ffn_kernel, ff_chunks=ff_chunks),
        grid=(t // tm,),
        in_specs=[pl.BlockSpec((tm, d), lambda i: (i, 0))] + [_const_spec(c.shape) for c in consts],
        out_specs=pl.BlockSpec((tm, d), lambda i: (i, 0)),
        out_shape=jax.ShapeDtypeStruct((t, d), F32),
        compiler_params=_params("parallel"),
        name="ffn",
    )(x2d, *consts)


def _largest_tile(n, cap):
    t = cap
    while n % t:
        t //= 2
    return t


def _trunk(x, mem, norm_mix_w, w_in, conv_w, dn_a_log, dn_dt_bias, dn_norm_w, w_up_a, sg_ln_w, sg_ln_b,
           sg_w, sg_b, w_up_b, w_out, norm_xa_w, norm_mem_w, xa_w_q, xa_w_kv, xa_w_o, norm_ffn_w,
           ffn_w_gate_up, ffn_w_down, final_norm_w):
    bsz, seq, d = x.shape
    t = bsz * seq
    assert seq % SG_CHUNK == 0 and seq % DN_CHUNK == 0 and LANES % DN_CHUNK == 0
    tm = _largest_tile(seq, ROW_TILE)
    cb = _largest_tile(seq, DELTA_BLOCK)
    x2d = x.reshape(t, d)
    qkv, dng, sgu, sgv, ga, gb, gates_t = _inproj(x2d, seq, norm_mix_w, w_in, conv_w, dn_a_log, dn_dt_bias, tm)
    o_f, o_b = _delta_scan(qkv.reshape(bsz, seq, -1), gates_t, cb)
    n_mem = mem.shape[1]
    kv = _norm_proj(mem.reshape(bsz * n_mem, d), norm_mem_w, xa_w_kv, _largest_tile(bsz * n_mem, MXU_COLS))
    x2d = _mix(x2d, o_f.reshape(t, DN_W), o_b.reshape(t, DN_W), dng, sgu, sgv, ga, gb, dn_norm_w, w_up_a,
               sg_ln_w, sg_ln_b, sg_w, sg_b, w_up_b, w_out, tm)
    x3d = _xattn(x2d.reshape(bsz, seq, d), kv.reshape(bsz, n_mem, 2 * d), norm_xa_w, xa_w_q, xa_w_o, tm)
    y = _ffn(x3d.reshape(t, d), norm_ffn_w, ffn_w_gate_up, ffn_w_down, final_norm_w, tm)
    return y.reshape(bsz, seq, d)


def kernel(x_prompt, x_sample, mem_prompt, mem_sample, norm_mix_w, w_in, conv_w, dn_a_log, dn_dt_bias, dn_norm_w, w_up_a, sg_ln_w, sg_ln_b, sg_w, sg_b, w_up_b, w_out, norm_xa_w, norm_mem_w, xa_w_q, xa_w_kv, xa_w_o, norm_ffn_w, ffn_w_gate_up, ffn_w_down, final_norm_w):
    depth = w_in.shape[0]
    assert depth == 1, "the FFN kernel fuses the final norm, which assumes a single layer"
    layer = (norm_mix_w[0], w_in[0], conv_w[0], dn_a_log[0], dn_dt_bias[0], dn_norm_w[0], w_up_a[0],
             sg_ln_w[0], sg_ln_b[0], sg_w[0], sg_b[0], w_up_b[0], w_out[0], norm_xa_w[0], norm_mem_w[0],
             xa_w_q[0], xa_w_kv[0], xa_w_o[0], norm_ffn_w[0], ffn_w_gate_up[0], ffn_w_down[0], final_norm_w)
    y_prompt = _trunk(x_prompt, mem_prompt, *layer)
    y_sample = _trunk(x_sample, mem_sample, *layer)
    return (y_prompt, y_sample)
```

```python
import functools

import jax
import jax.numpy as jnp
from jax import lax
from jax.experimental import pallas as pl
from jax.experimental.pallas import tpu as pltpu

F32 = jnp.float32
BF16 = jnp.bfloat16
EPS = 1e-6
LOG2_E = 1.4426950408889634

DN_HEADS = 4
DN_D = 128
DN_W = DN_HEADS * DN_D
DN_CHUNK = 64
CONV_K = 5
SG_GROUPS = 4
SG_CHUNK = 128
SG_W = SG_GROUPS * SG_CHUNK
XA_HEADS = 4

LANES = 128
MXU_COLS = 256
CONV_HALO_ROWS = 16
ROW_BLOCK = 64
DELTA_GROUP_STAGGER = 2
ROW_TILE = 512
FFN_TILE = 1024
DELTA_BLOCK = 1024
VMEM_LIMIT_BYTES = 56 * 1024 * 1024


def _dot(a, b):
    return jnp.dot(a.astype(BF16), b.astype(BF16), preferred_element_type=F32)


def _dot_nt(a, b):
    return lax.dot_general(a.astype(BF16), b.astype(BF16), (((1,), (1,)), ((), ())),
                           preferred_element_type=F32)


def _split_bf16(x):
    hi = x.astype(BF16)
    r = x - hi.astype(F32)
    mid = r.astype(BF16)
    lo = (r - mid.astype(F32)).astype(BF16)
    return hi, mid, lo


def _cumsum_lanes(x, mask01):
    m, n = x.shape
    w = mask01.shape[0]
    parts = jnp.concatenate(_split_bf16(x), axis=0)
    stacked = jnp.concatenate([parts[:, j:j + w] for j in range(0, n, w)], axis=0)
    res = _dot(stacked, mask01)
    blocks = [res[j:j + 3 * m] for j in range(0, res.shape[0], 3 * m)]
    return jnp.concatenate([b[:m] + (b[m:2 * m] + b[2 * m:]) for b in blocks], axis=1)


def _zero_after(x):
    bits = pltpu.bitcast(x[0:8, 0:LANES], jnp.uint32)
    sixteen = jnp.uint32(16)
    return lax.shift_right_logical(lax.shift_right_logical(bits, sixteen), sixteen).astype(F32)


def _rms(x, w):
    return x * lax.rsqrt(jnp.mean(x * x, -1, keepdims=True) + EPS) * w


def _silu(x):
    h = 0.5 * x
    return h + h * jnp.tanh(h)


def _gelu_tanh(x):
    c = 0.7978845608028654
    h = 0.5 * x
    return h + h * jnp.tanh(x * (c + (c * 0.044715) * (x * x)))


def _sigmoid_mix(ga, ya, gb, yb):
    return 0.5 * ((ya + yb) + (jnp.tanh(0.5 * ga) * ya + jnp.tanh(0.5 * gb) * yb))


def _params(*sem):
    return pltpu.CompilerParams(dimension_semantics=sem, vmem_limit_bytes=VMEM_LIMIT_BYTES)


def _const_spec(shape):
    nd = len(shape)
    return pl.BlockSpec(shape, lambda *_: (0,) * nd)


def _run_staggered(gens, stagger):
    waiting = list(gens)
    live = []
    tick = 0
    while waiting or live:
        if waiting and tick % stagger == 0:
            live.append(waiting.pop(0))
        for g in list(live):
            try:
                next(g)
            except StopIteration:
                live.remove(g)
        tick += 1


def _gate_math(ab, a_log, dt_bias, is_beta):
    z = ab + dt_bias
    softplus = jnp.maximum(z, 0.0) + jnp.log1p(jnp.exp(-jnp.abs(z)))
    return jnp.where(is_beta, jax.nn.sigmoid(ab), -jnp.exp(a_log) * softplus)


def _inproj_kernel(x0_ref, xnext_ref, xp_ref, xn_ref, nw_ref, wqkv_ref, cw_ref, wdng_ref, wsgu_ref, wsgv_ref,
                   wga_ref, wgb_ref, wabt_ref, alog_ref, dtb_ref,
                   qkv_ref, dng_ref, sgu_ref, sgv_ref, ga_ref, gb_ref, gatet_ref, raw_ref, h_ref, hnext_ref,
                   *, tiles_per_seq):
    tm = hnext_ref.shape[0]
    h0 = CONV_HALO_ROWS
    pad = CONV_K // 2
    pos = pl.program_id(0) % tiles_per_seq
    half = tm // 2

    @pl.when(pl.program_id(0) == 0)
    def _():
        hnext_ref[...] = _rms(x0_ref[...], nw_ref[...]).astype(BF16)

    h_ref[h0:h0 + tm, :] = hnext_ref[...]
    h_ref[0:h0, :] = _rms(xp_ref[...], nw_ref[...]).astype(BF16)
    h_ref[h0 + tm:2 * h0 + tm, :] = _rms(xn_ref[...], nw_ref[...]).astype(BF16)
    keep_prev = jnp.where(pos > 0, 1.0, 0.0)
    keep_next = jnp.where(pos < tiles_per_seq - 1, 1.0, 0.0)

    def qkv_chunk(j):
        res = _dot(h_ref[...], wqkv_ref[:, j:j + MXU_COLS])
        raw_ref[:, j:j + MXU_COLS] = res
        raw_ref[0:h0, j:j + MXU_COLS] = res[0:h0] * keep_prev
        raw_ref[h0 + tm:2 * h0 + tm, j:j + MXU_COLS] = res[h0 + tm:2 * h0 + tm] * keep_next

    def conv_unit(g, r, after):
        cols = slice(g * DN_D, (g + 1) * DN_D)
        lo = h0 - pad + r
        acc = raw_ref[lo:lo + ROW_BLOCK, cols] * (cw_ref[0:1, cols] + after[0:1, :])
        for j in range(1, CONV_K):
            acc = acc + raw_ref[lo + j:lo + j + ROW_BLOCK, cols] * cw_ref[j:j + 1, cols]
        y = _silu(acc)
        if g < 2 * DN_HEADS:
            y = y * lax.rsqrt(jnp.sum(y * y, -1, keepdims=True) + EPS)
        if g < DN_HEADS:
            y = y * (DN_D ** -0.5)
        qkv_ref[r:r + ROW_BLOCK, cols] = y.astype(BF16)

    rest = [(w_ref, o_ref, j, r0)
            for w_ref, o_ref in ((wdng_ref, dng_ref), (wsgu_ref, sgu_ref), (wsgv_ref, sgv_ref),
                                 (wga_ref, ga_ref), (wgb_ref, gb_ref))
            for j in range(0, w_ref.shape[1], MXU_COLS) for r0 in (0, half)]
    n_rest = len(rest)

    def rest_unit(w_ref, o_ref, j, r0):
        res = _dot(h_ref[h0 + r0:h0 + r0 + half, :], w_ref[:, j:j + MXU_COLS])
        o_ref[r0:r0 + half, j:j + MXU_COLS] = res.astype(o_ref.dtype)
        return _zero_after(res)

    norm_rows = ROW_BLOCK // 2
    norm_units = list(range(0, tm, norm_rows))
    n_norm = len(norm_units)

    def norm_unit(r, after):
        nw = nw_ref[...] + jnp.concatenate([after[0:1, :]] * (nw_ref.shape[1] // LANES), axis=1)
        hnext_ref[r:r + norm_rows, :] = _rms(xnext_ref[r:r + norm_rows, :], nw).astype(BF16)

    groups_per_chunk = MXU_COLS // DN_D
    n_chunks = 3 * DN_W // MXU_COLS
    n_conv = 3 * DN_HEADS * (tm // ROW_BLOCK)
    done = 0
    after = jnp.zeros((8, LANES), F32)
    qkv_chunk(0)
    for ch in range(n_chunks):
        if ch + 1 < n_chunks:
            qkv_chunk((ch + 1) * MXU_COLS)
        for g in range(ch * groups_per_chunk, (ch + 1) * groups_per_chunk):
            for r in range(0, tm, ROW_BLOCK):
                conv_unit(g, r, after)
                done += 1
                while rest and (n_rest - len(rest)) * n_conv < done * n_rest:
                    after = rest_unit(*rest.pop(0))
                while norm_units and (n_norm - len(norm_units)) * n_conv < done * n_norm:
                    norm_unit(norm_units.pop(0), after)
    while rest:
        rest_unit(*rest.pop(0))
    abt = _dot_nt(wabt_ref[...], h_ref[h0:h0 + tm, :])
    row = lax.broadcasted_iota(jnp.int32, abt.shape, 0)
    gatet_ref[...] = _gate_math(abt, alog_ref[...], dtb_ref[...], row < 2 * DN_HEADS)


def _inproj(x2d, seq, norm_w, w_in, conv_w, a_log, dt_bias, tm):
    t, d = x2d.shape
    hb = tm // CONV_HALO_ROWS
    last_halo = t // CONV_HALO_ROWS - 1
    n_qkv = 3 * DN_W
    n_ab = 4 * DN_HEADS
    bounds = [0, n_qkv, n_qkv + n_ab]
    for width in (DN_W, SG_W, SG_W, d, d):
        bounds.append(bounds[-1] + width)
    assert bounds[-1] == w_in.shape[1]
    piece = lambda k: w_in[:, bounds[k]:bounds[k + 1]].astype(BF16)
    w_qkv, w_ab = piece(0), piece(1)
    w_rest = [piece(k) for k in range(2, 7)]
    zeros8 = jnp.zeros((2 * DN_HEADS,), F32)
    alog16 = jnp.concatenate([zeros8, a_log.reshape(-1).astype(F32)])
    dtb16 = jnp.concatenate([zeros8, dt_bias.reshape(-1).astype(F32)])
    widths = [n_qkv, DN_W, SG_W, SG_W, d, d]
    out_shape = [jax.ShapeDtypeStruct((t, n), BF16) for n in widths]
    out_shape += [jax.ShapeDtypeStruct((n_ab, t), F32)]
    out_specs = [pl.BlockSpec((tm, n), lambda i: (i, 0)) for n in widths]
    out_specs += [pl.BlockSpec((n_ab, tm), lambda i: (0, i))]
    consts = [norm_w.reshape(1, d).astype(F32), w_qkv, conv_w.astype(F32), *w_rest, w_ab.T,
              alog16.reshape(n_ab, 1), dtb16.reshape(n_ab, 1)]
    n_tiles = t // tm
    x_specs = [pl.BlockSpec((tm, d), lambda i: (0, 0)),
               pl.BlockSpec((tm, d), lambda i: (jnp.minimum(i + 1, n_tiles - 1), 0)),
               pl.BlockSpec((CONV_HALO_ROWS, d), lambda i: (jnp.maximum(i * hb - 1, 0), 0)),
               pl.BlockSpec((CONV_HALO_ROWS, d), lambda i: (jnp.minimum((i + 1) * hb, last_halo), 0))]
    return pl.pallas_call(
        functools.partial(_inproj_kernel, tiles_per_seq=seq // tm),
        grid=(n_tiles,),
        in_specs=x_specs + [_const_spec(c.shape) for c in consts],
        out_specs=out_specs,
        out_shape=out_shape,
        scratch_shapes=[pltpu.VMEM((tm + 2 * CONV_HALO_ROWS, n_qkv), F32),
                        pltpu.VMEM((tm + 2 * CONV_HALO_ROWS, d), BF16), pltpu.VMEM((tm, d), BF16)],
        compiler_params=_params("arbitrary"),
        name="inproj",
    )(x2d, x2d, x2d, x2d, *consts)


def _delta_kernel(qf_ref, gtf_ref, qb_ref, gtb_ref, of_ref, ob_ref, state_ref, *, cb):
    c = DN_CHUNK
    nc = cb // c
    qkv_refs = (qf_ref, qb_ref)
    n_gate = gtf_ref.shape[0]

    @pl.when(pl.program_id(1) == 0)
    def _():
        state_ref[...] = jnp.zeros_like(state_ref)

    ri = lax.broadcasted_iota(jnp.int32, (LANES, LANES), 0)
    ci = lax.broadcasted_iota(jnp.int32, (LANES, LANES), 1)
    same = (ri & -c) == (ci & -c)
    ge = jnp.where(same & (ri >= ci), 1.0, 0.0)
    le = jnp.where(same & (ri <= ci), 1.0, 0.0)
    gates_t = (gtf_ref[...], gtb_ref[...])
    gc_t = (_cumsum_lanes(gates_t[0], le) * LOG2_E, _cumsum_lanes(gates_t[1], ge) * LOG2_E)
    is_beta = lax.broadcasted_iota(jnp.int32, (n_gate, cb), 0) < 2 * DN_HEADS
    pad_rows = jnp.zeros((LANES - n_gate, cb), F32)
    gc = tuple(jnp.concatenate([jnp.where(is_beta, gates_t[d], gc_t[d]), pad_rows], axis=0).T for d in range(2))

    ri2 = lax.broadcasted_iota(jnp.int32, (c, 2 * c), 0)
    lane2 = lax.broadcasted_iota(jnp.int32, (c, 2 * c), 1)
    left = lane2 < c
    eye2 = jnp.where(ri2 == (lane2 & (c - 1)), 1.0, 0.0)
    ri1 = lax.broadcasted_iota(jnp.int32, (c, c), 0)
    ci1 = lax.broadcasted_iota(jnp.int32, (c, c), 1)
    incl = (ri1 >= ci1, ri1 <= ci1)
    strict = (ri1 > ci1, ri1 < ci1)
    zeros1 = jnp.zeros((c, c), F32)

    state = {(d, h): state_ref[d, h] for d in range(2) for h in range(DN_HEADS)}
    steps_done = [0]
    o_refs = (of_ref, ob_ref)

    def operand(col0, it):
        d, n, h = it
        return qkv_refs[d][n * c:(n + 1) * c, col0 + h * DN_D:col0 + (h + 1) * DN_D]

    def group(s):
        items = [(d, (nc - 1 - s) if d else s, h) for d in range(2) for h in range(DN_HEADS)]
        q = functools.partial(operand, 0)
        k = functools.partial(operand, DN_W)
        v = functools.partial(operand, 2 * DN_W)
        beta, g_col, g_last, decay = {}, {}, {}, {}
        qkk = {it: _dot_nt(jnp.concatenate([q(it), k(it)], axis=0), k(it)) for it in items}
        yield
        for it in items:
            d, n, h = it
            rows = slice(n * c, (n + 1) * c)
            bcol = d * DN_HEADS + h
            gcol = (2 + d) * DN_HEADS + h
            beta[it] = jnp.broadcast_to(gc[d][rows, bcol:bcol + 1], (c, DN_D))
            g_col[it] = jnp.broadcast_to(gc[d][rows, gcol:gcol + 1], (c, DN_D))
            g_row = gc_t[d][gcol:gcol + 1, rows]
            g_last[it] = g_col[it][0:1, :] if d else g_col[it][c - 1:c, :]
            decay[it] = jnp.exp2(g_col[it][:, :c] - g_row)
        yield
        a_intra = {it: jnp.where(incl[it[0]], qkk[it][:c] * decay[it], 0.0) for it in items}
        lm = {it: jnp.where(strict[it[0]], beta[it][:, :c] * qkk[it][c:] * decay[it], 0.0) for it in items}
        lm_l = {it: jnp.concatenate([lm[it], zeros1], axis=1) for it in items}
        sq = {it: _dot(lm[it], lm_l[it]) for it in items}
        yield
        z = {it: jnp.where(left, sq[it], eye2 - pltpu.roll(lm_l[it], c, 1)) for it in items}
        eg = {it: jnp.exp2(g_col[it]) for it in items}
        uw_rhs = {it: jnp.concatenate([(v(it).astype(F32) * beta[it]).astype(BF16),
                                       (k(it).astype(F32) * (beta[it] * eg[it])).astype(BF16)], axis=1)
                  for it in items}
        power = 2
        while power < c:
            zb = {it: z[it].astype(BF16) for it in items}
            r = {it: _dot(zb[it][:, :c], zb[it]) for it in items}
            yield
            keep = ~left if 2 * power < c else True
            z = {it: jnp.where(keep, z[it], 0.0) + r[it] for it in items}
            power *= 2
        uw = {it: _dot(pltpu.roll(z[it], c, 1)[:, :c], uw_rhs[it]) for it in items}
        kga = {it: jnp.concatenate([(k(it).astype(F32) * jnp.exp2(g_last[it] - g_col[it])).T.astype(BF16),
                                    a_intra[it].astype(BF16)], axis=0) for it in items}
        yield
        assert steps_done[0] == s, "recurrence steps must be issued in order"
        wq = {it: jnp.concatenate([uw[it][:, DN_D:], q(it).astype(F32) * eg[it]], axis=0) for it in items}
        ws_qs = {it: _dot(wq[it], state[it[0], it[2]]) for it in items}
        yield
        v_new = {it: uw[it][:, :DN_D] - ws_qs[it][:c] for it in items}
        upd = {it: _dot(kga[it], v_new[it]) for it in items}
        yield
        for it in items:
            d, n, h = it
            state[d, h] = state[d, h] * jnp.exp2(g_last[it]) + upd[it][:DN_D]
            o = ws_qs[it][c:] + upd[it][DN_D:]
            o_refs[d][n * c:(n + 1) * c, h * DN_D:(h + 1) * DN_D] = o.astype(BF16)
        steps_done[0] += 1

    _run_staggered([group(s) for s in range(nc)], DELTA_GROUP_STAGGER)
    for (d, h), val in state.items():
        state_ref[d, h] = val


def _delta_scan(qkv, gates_t, cb):
    bsz, seq, n_qkv = qkv.shape
    nb = seq // cb
    n_gate = gates_t.shape[0]

    def stream_specs(blk):
        return [
            pl.BlockSpec((None, cb, n_qkv), lambda b, i: (b, blk(i), 0)),
            pl.BlockSpec((n_gate, cb), lambda b, i: (0, b * nb + blk(i))),
        ]

    fwd = lambda i: i
    bwd = lambda i: nb - 1 - i
    stream = (qkv, gates_t)
    return pl.pallas_call(
        functools.partial(_delta_kernel, cb=cb),
        grid=(bsz, nb),
        in_specs=stream_specs(fwd) + stream_specs(bwd),
        out_specs=[pl.BlockSpec((None, cb, DN_W), lambda b, i: (b, fwd(i), 0)),
                   pl.BlockSpec((None, cb, DN_W), lambda b, i: (b, bwd(i), 0))],
        out_shape=[jax.ShapeDtypeStruct((bsz, seq, DN_W), BF16)] * 2,
        scratch_shapes=[pltpu.VMEM((2, DN_HEADS, DN_D, DN_D), F32)],
        compiler_params=_params("parallel", "arbitrary"),
        name="delta",
    )(*stream, *stream)


def _mix_kernel(x_ref, of_ref, ob_ref, dng_ref, sgu_ref, sgv_ref, ga_ref, gb_ref,
                dnw_ref, wua_ref, lnw_ref, lnb_ref, ws_ref, bst_ref, wub_ref, wout_ref,
                y_ref, dn_s, sg_s):
    tm = x_ref.shape[0]
    for h in range(DN_HEADS):
        cols = slice(h * DN_D, (h + 1) * DN_D)
        o = of_ref[:, cols].astype(F32) + ob_ref[:, cols].astype(F32)
        dn_s[:, cols] = (_rms(o, dnw_ref[...]) * _silu(dng_ref[:, cols].astype(F32))).astype(BF16)
    y_a = _dot(dn_s[...], wua_ref[...])
    v = _gelu_tanh(sgv_ref[...].astype(F32))
    mu = jnp.mean(v, -1, keepdims=True)
    vc = v - mu
    v = vc * lax.rsqrt(jnp.mean(vc * vc, -1, keepdims=True) + EPS) * lnw_ref[...] + lnb_ref[...]
    v = v.astype(BF16)
    for n in range(tm // SG_CHUNK):
        rows = slice(n * SG_CHUNK, (n + 1) * SG_CHUNK)
        for g in range(SG_GROUPS):
            cols = slice(g * SG_CHUNK, (g + 1) * SG_CHUNK)
            mixed = _dot(ws_ref[g], v[rows, cols]) + bst_ref[:, g:g + 1]
            sg_s[rows, cols] = (_gelu_tanh(sgu_ref[rows, cols].astype(F32)) * mixed).astype(BF16)
    y_b = _dot(sg_s[...], wub_ref[...])
    merged = _sigmoid_mix(ga_ref[...].astype(F32), y_a, gb_ref[...].astype(F32), y_b)
    y_ref[...] = x_ref[...] + _dot(merged, wout_ref[...])


def _mix(x2d, o_f, o_b, dng, sgu, sgv, ga, gb, dn_norm_w, w_up_a, ln_w, ln_b, sg_w, sg_b, w_up_b, w_out, tm):
    t, d = x2d.shape
    consts = [dn_norm_w.reshape(1, DN_D).astype(F32), w_up_a.astype(BF16),
              ln_w.reshape(1, SG_W).astype(F32), ln_b.reshape(1, SG_W).astype(F32),
              sg_w.astype(BF16), sg_b.T.astype(F32), w_up_b.astype(BF16), w_out.astype(BF16)]
    acts = [x2d, o_f, o_b, dng, sgu, sgv, ga, gb]
    return pl.pallas_call(
        _mix_kernel,
        grid=(t // tm,),
        in_specs=[pl.BlockSpec((tm, a.shape[1]), lambda i: (i, 0)) for a in acts]
                 + [_const_spec(c.shape) for c in consts],
        out_specs=pl.BlockSpec((tm, d), lambda i: (i, 0)),
        out_shape=jax.ShapeDtypeStruct((t, d), F32),
        scratch_shapes=[pltpu.VMEM((tm, DN_W), BF16), pltpu.VMEM((tm, SG_W), BF16)],
        compiler_params=_params("parallel"),
        name="mix",
    )(*acts, *consts)


def _norm_proj_kernel(x_ref, nw_ref, w_ref, o_ref):
    o_ref[...] = _dot(_rms(x_ref[...], nw_ref[...]), w_ref[...]).astype(o_ref.dtype)


def _norm_proj(x2d, norm_w, w, tm):
    t, d = x2d.shape
    n = w.shape[1]
    return pl.pallas_call(
        _norm_proj_kernel,
        grid=(t // tm,),
        in_specs=[pl.BlockSpec((tm, d), lambda i: (i, 0)), _const_spec((1, d)), _const_spec(w.shape)],
        out_specs=pl.BlockSpec((tm, n), lambda i: (i, 0)),
        out_shape=jax.ShapeDtypeStruct((t, n), BF16),
        compiler_params=_params("parallel"),
        name="mem_kv",
    )(x2d, norm_w.reshape(1, d).astype(F32), w.astype(BF16))


def _xattn_kernel(x_ref, kv_ref, nw_ref, wq_ref, wo_ref, y_ref, o_s):
    d = x_ref.shape[1]
    hd = d // XA_HEADS
    x = x_ref[...]
    q = _dot(_rms(x, nw_ref[...]), wq_ref[...])
    for h in range(XA_HEADS):
        cols = slice(h * hd, (h + 1) * hd)
        s = _dot_nt(q[:, cols], kv_ref[:, cols]) * (hd ** -0.5)
        p = jnp.exp(s - jnp.max(s, -1, keepdims=True))
        p = p / jnp.sum(p, -1, keepdims=True)
        o_s[:, cols] = _dot(p, kv_ref[:, d + h * hd:d + (h + 1) * hd]).astype(BF16)
    y_ref[...] = x + _dot(o_s[...], wo_ref[...])


def _xattn(x3d, kv, norm_w, w_q, w_o, tm):
    bsz, seq, d = x3d.shape
    n_mem = kv.shape[1]
    consts = [norm_w.reshape(1, d).astype(F32), w_q.astype(BF16), w_o.astype(BF16)]
    return pl.pallas_call(
        _xattn_kernel,
        grid=(bsz, seq // tm),
        in_specs=[pl.BlockSpec((None, tm, d), lambda b, i: (b, i, 0)),
                  pl.BlockSpec((None, n_mem, 2 * d), lambda b, i: (b, 0, 0))]
                 + [_const_spec(c.shape) for c in consts],
        out_specs=pl.BlockSpec((None, tm, d), lambda b, i: (b, i, 0)),
        out_shape=jax.ShapeDtypeStruct((bsz, seq, d), F32),
        scratch_shapes=[pltpu.VMEM((tm, d), BF16)],
        compiler_params=_params("parallel", "parallel"),
        name="xattn",
    )(x3d, kv, *consts)


def _ffn_kernel(x_ref, nw_ref, wg_ref, wu_ref, wd_ref, fw_ref, y_ref, *, ff_chunks):
    x = x_ref[...]
    h = _rms(x, nw_ref[...]).astype(BF16)
    acc = x
    for lo, hi in ff_chunks:
        a = _silu(_dot(h, wg_ref[:, lo:hi])) * _dot(h, wu_ref[:, lo:hi])
        acc = acc + _dot(a, wd_ref[lo:hi, :])
    y_ref[...] = _rms(acc, fw_ref[...])


def _ffn(x2d, norm_w, w_gate_up, w_down, final_w, tm):
    t, d = x2d.shape
    d_ff = w_down.shape[0]
    assert d_ff % MXU_COLS == 0
    ff_step = 3 * MXU_COLS
    ff_chunks = tuple((lo, min(lo + ff_step, d_ff)) for lo in range(0, d_ff, ff_step))
    consts = [norm_w.reshape(1, d).astype(F32), w_gate_up[:, :d_ff].astype(BF16),
              w_gate_up[:, d_ff:].astype(BF16), w_down.astype(BF16), final_w.reshape(1, d).astype(F32)]
    return pl.pallas_call(
        functools.partial(_ffn_kernel, ff_chunks=ff_chunks),
        grid=(t // tm,),
        in_specs=[pl.BlockSpec((tm, d), lambda i: (i, 0))] + [_const_spec(c.shape) for c in consts],
        out_specs=pl.BlockSpec((tm, d), lambda i: (i, 0)),
        out_shape=jax.ShapeDtypeStruct((t, d), F32),
        compiler_params=_params("parallel"),
        name="ffn",
    )(x2d, *consts)


def _largest_tile(n, cap):
    t = cap
    while n % t:
        t //= 2
    return t


def _trunk(x, mem, norm_mix_w, w_in, conv_w, dn_a_log, dn_dt_bias, dn_norm_w, w_up_a, sg_ln_w, sg_ln_b,
           sg_w, sg_b, w_up_b, w_out, norm_xa_w, norm_mem_w, xa_w_q, xa_w_kv, xa_w_o, norm_ffn_w,
           ffn_w_gate_up, ffn_w_down, final_norm_w):
    bsz, seq, d = x.shape
    t = bsz * seq
    assert seq % SG_CHUNK == 0 and seq % DN_CHUNK == 0 and LANES % DN_CHUNK == 0
    tm = _largest_tile(seq, ROW_TILE)
    cb = _largest_tile(seq, DELTA_BLOCK)
    x2d = x.reshape(t, d)
    qkv, dng, sgu, sgv, ga, gb, gates_t = _inproj(x2d, seq, norm_mix_w, w_in, conv_w, dn_a_log, dn_dt_bias, tm)
    o_f, o_b = _delta_scan(qkv.reshape(bsz, seq, -1), gates_t, cb)
    n_mem = mem.shape[1]
    kv = _norm_proj(mem.reshape(bsz * n_mem, d), norm_mem_w, xa_w_kv, _largest_tile(bsz * n_mem, MXU_COLS))
    x2d = _mix(x2d, o_f.reshape(t, DN_W), o_b.reshape(t, DN_W), dng, sgu, sgv, ga, gb, dn_norm_w, w_up_a,
               sg_ln_w, sg_ln_b, sg_w, sg_b, w_up_b, w_out, tm)
    x3d = _xattn(x2d.reshape(bsz, seq, d), kv.reshape(bsz, n_mem, 2 * d), norm_xa_w, xa_w_q, xa_w_o, tm)
    y = _ffn(x3d.reshape(t, d), norm_ffn_w, ffn_w_gate_up, ffn_w_down, final_norm_w, _largest_tile(seq, FFN_TILE))
    return y.reshape(bsz, seq, d)


def kernel(x_prompt, x_sample, mem_prompt, mem_sample, norm_mix_w, w_in, conv_w, dn_a_log, dn_dt_bias, dn_norm_w, w_up_a, sg_ln_w, sg_ln_b, sg_w, sg_b, w_up_b, w_out, norm_xa_w, norm_mem_w, xa_w_q, xa_w_kv, xa_w_o, norm_ffn_w, ffn_w_gate_up, ffn_w_down, final_norm_w):
    depth = w_in.shape[0]
    assert depth == 1, "the FFN kernel fuses the final norm, which assumes a single layer"
    layer = (norm_mix_w[0], w_in[0], conv_w[0], dn_a_log[0], dn_dt_bias[0], dn_norm_w[0], w_up_a[0],
             sg_ln_w[0], sg_ln_b[0], sg_w[0], sg_b[0], w_up_b[0], w_out[0], norm_xa_w[0], norm_mem_w[0],
             xa_w_q[0], xa_w_kv[0], xa_w_o[0], norm_ffn_w[0], ffn_w_gate_up[0], ffn_w_down[0], final_norm_w)
    y_prompt = _trunk(x_prompt, mem_prompt, *layer)
    y_sample = _trunk(x_sample, mem_sample, *layer)
    return (y_prompt, y_sample)
```

```python
import functools

import jax
import jax.numpy as jnp
from jax import lax
from jax.experimental import pallas as pl
from jax.experimental.pallas import tpu as pltpu

F32 = jnp.float32
BF16 = jnp.bfloat16
EPS = 1e-6
LOG2_E = 1.4426950408889634

DN_HEADS = 4
DN_D = 128
DN_W = DN_HEADS * DN_D
DN_CHUNK = 64
CONV_K = 5
SG_GROUPS = 4
SG_CHUNK = 128
SG_W = SG_GROUPS * SG_CHUNK
XA_HEADS = 4

LANES = 128
MXU_COLS = 256
CONV_HALO_ROWS = 16
ROW_BLOCK = 64
DELTA_GROUP_STAGGER = 2
ROW_TILE = 512
FFN_TILE = 1024
DELTA_BLOCK = 1024
VMEM_LIMIT_BYTES = 56 * 1024 * 1024


def _dot(a, b):
    return jnp.dot(a.astype(BF16), b.astype(BF16), preferred_element_type=F32)


def _dot_nt(a, b):
    return lax.dot_general(a.astype(BF16), b.astype(BF16), (((1,), (1,)), ((), ())),
                           preferred_element_type=F32)


def _split_bf16(x):
    hi = x.astype(BF16)
    r = x - hi.astype(F32)
    mid = r.astype(BF16)
    lo = (r - mid.astype(F32)).astype(BF16)
    return hi, mid, lo


def _cumsum_lanes(x, mask01):
    m, n = x.shape
    w = mask01.shape[0]
    parts = jnp.concatenate(_split_bf16(x), axis=0)
    stacked = jnp.concatenate([parts[:, j:j + w] for j in range(0, n, w)], axis=0)
    res = _dot(stacked, mask01)
    blocks = [res[j:j + 3 * m] for j in range(0, res.shape[0], 3 * m)]
    return jnp.concatenate([b[:m] + (b[m:2 * m] + b[2 * m:]) for b in blocks], axis=1)


def _zero_after(x):
    bits = pltpu.bitcast(x[0:8, 0:LANES], jnp.uint32)
    sixteen = jnp.uint32(16)
    return lax.shift_right_logical(lax.shift_right_logical(bits, sixteen), sixteen).astype(F32)


def _rms(x, w):
    return x * lax.rsqrt(jnp.mean(x * x, -1, keepdims=True) + EPS) * w


def _silu(x):
    h = 0.5 * x
    return h + h * jnp.tanh(h)


def _gelu_tanh(x):
    c = 0.7978845608028654
    h = 0.5 * x
    return h + h * jnp.tanh(x * (c + (c * 0.044715) * (x * x)))


def _sigmoid_mix(ga, ya, gb, yb):
    return 0.5 * ((ya + yb) + (jnp.tanh(0.5 * ga) * ya + jnp.tanh(0.5 * gb) * yb))


def _params(*sem):
    return pltpu.CompilerParams(dimension_semantics=sem, vmem_limit_bytes=VMEM_LIMIT_BYTES)


def _const_spec(shape):
    nd = len(shape)
    return pl.BlockSpec(shape, lambda *_: (0,) * nd)


def _run_staggered(gens, stagger):
    waiting = list(gens)
    live = []
    tick = 0
    while waiting or live:
        if waiting and tick % stagger == 0:
            live.append(waiting.pop(0))
        for g in list(live):
            try:
                next(g)
            except StopIteration:
                live.remove(g)
        tick += 1


def _gate_math(ab, a_log, dt_bias, is_beta):
    z = ab + dt_bias
    softplus = jnp.maximum(z, 0.0) + jnp.log1p(jnp.exp(-jnp.abs(z)))
    return jnp.where(is_beta, jax.nn.sigmoid(ab), -jnp.exp(a_log) * softplus)


def _inproj_kernel(x0_ref, xnext_ref, xp_ref, xn_ref, nw_ref, wqkv_ref, cw_ref, wdng_ref, wsgu_ref, wsgv_ref,
                   wga_ref, wgb_ref, wabt_ref, alog_ref, dtb_ref,
                   qkv_ref, dng_ref, sgu_ref, sgv_ref, ga_ref, gb_ref, gatet_ref, raw_ref, h_ref, hnext_ref,
                   *, tiles_per_seq):
    tm = hnext_ref.shape[0]
    h0 = CONV_HALO_ROWS
    pad = CONV_K // 2
    pos = pl.program_id(0) % tiles_per_seq
    half = tm // 2

    @pl.when(pl.program_id(0) == 0)
    def _():
        hnext_ref[...] = _rms(x0_ref[...], nw_ref[...]).astype(BF16)

    h_ref[h0:h0 + tm, :] = hnext_ref[...]
    h_ref[0:h0, :] = _rms(xp_ref[...], nw_ref[...]).astype(BF16)
    h_ref[h0 + tm:2 * h0 + tm, :] = _rms(xn_ref[...], nw_ref[...]).astype(BF16)
    keep_prev = jnp.where(pos > 0, 1.0, 0.0)
    keep_next = jnp.where(pos < tiles_per_seq - 1, 1.0, 0.0)

    def qkv_chunk(j):
        res = _dot(h_ref[...], wqkv_ref[:, j:j + MXU_COLS])
        raw_ref[:, j:j + MXU_COLS] = res
        raw_ref[0:h0, j:j + MXU_COLS] = res[0:h0] * keep_prev
        raw_ref[h0 + tm:2 * h0 + tm, j:j + MXU_COLS] = res[h0 + tm:2 * h0 + tm] * keep_next

    def conv_unit(g, r, after):
        cols = slice(g * DN_D, (g + 1) * DN_D)
        lo = h0 - pad + r
        acc = raw_ref[lo:lo + ROW_BLOCK, cols] * (cw_ref[0:1, cols] + after[0:1, :])
        for j in range(1, CONV_K):
            acc = acc + raw_ref[lo + j:lo + j + ROW_BLOCK, cols] * cw_ref[j:j + 1, cols]
        y = _silu(acc)
        if g < 2 * DN_HEADS:
            y = y * lax.rsqrt(jnp.sum(y * y, -1, keepdims=True) + EPS)
        if g < DN_HEADS:
            y = y * (DN_D ** -0.5)
        qkv_ref[r:r + ROW_BLOCK, cols] = y.astype(BF16)

    rest = [(w_ref, o_ref, j, r0)
            for w_ref, o_ref in ((wdng_ref, dng_ref), (wsgu_ref, sgu_ref), (wsgv_ref, sgv_ref),
                                 (wga_ref, ga_ref), (wgb_ref, gb_ref))
            for j in range(0, w_ref.shape[1], MXU_COLS) for r0 in (0, half)]
    n_rest = len(rest)

    def rest_unit(w_ref, o_ref, j, r0):
        res = _dot(h_ref[h0 + r0:h0 + r0 + half, :], w_ref[:, j:j + MXU_COLS])
        o_ref[r0:r0 + half, j:j + MXU_COLS] = res.astype(o_ref.dtype)
        return _zero_after(res)

    norm_rows = ROW_BLOCK // 2
    norm_units = list(range(0, tm, norm_rows))
    n_norm = len(norm_units)

    def norm_unit(r, after):
        nw = nw_ref[...] + jnp.concatenate([after[0:1, :]] * (nw_ref.shape[1] // LANES), axis=1)
        hnext_ref[r:r + norm_rows, :] = _rms(xnext_ref[r:r + norm_rows, :], nw).astype(BF16)

    groups_per_chunk = MXU_COLS // DN_D
    n_chunks = 3 * DN_W // MXU_COLS
    n_conv = 3 * DN_HEADS * (tm // ROW_BLOCK)
    done = 0
    after = jnp.zeros((8, LANES), F32)
    qkv_chunk(0)
    for ch in range(n_chunks):
        if ch + 1 < n_chunks:
            qkv_chunk((ch + 1) * MXU_COLS)
        for g in range(ch * groups_per_chunk, (ch + 1) * groups_per_chunk):
            for r in range(0, tm, ROW_BLOCK):
                conv_unit(g, r, after)
                done += 1
                while rest and (n_rest - len(rest)) * n_conv < done * n_rest:
                    after = rest_unit(*rest.pop(0))
                while norm_units and (n_norm - len(norm_units)) * n_conv < done * n_norm:
                    norm_unit(norm_units.pop(0), after)
    while rest:
        rest_unit(*rest.pop(0))
    abt = _dot_nt(wabt_ref[...], h_ref[h0:h0 + tm, :])
    row = lax.broadcasted_iota(jnp.int32, abt.shape, 0)
    gatet_ref[...] = _gate_math(abt, alog_ref[...], dtb_ref[...], row < 2 * DN_HEADS)


def _inproj(x2d, seq, norm_w, w_in, conv_w, a_log, dt_bias, tm):
    t, d = x2d.shape
    hb = tm // CONV_HALO_ROWS
    last_halo = t // CONV_HALO_ROWS - 1
    n_qkv = 3 * DN_W
    n_ab = 4 * DN_HEADS
    bounds = [0, n_qkv, n_qkv + n_ab]
    for width in (DN_W, SG_W, SG_W, d, d):
        bounds.append(bounds[-1] + width)
    assert bounds[-1] == w_in.shape[1]
    piece = lambda k: w_in[:, bounds[k]:bounds[k + 1]].astype(BF16)
    w_qkv, w_ab = piece(0), piece(1)
    w_rest = [piece(k) for k in range(2, 7)]
    zeros8 = jnp.zeros((2 * DN_HEADS,), F32)
    alog16 = jnp.concatenate([zeros8, a_log.reshape(-1).astype(F32)])
    dtb16 = jnp.concatenate([zeros8, dt_bias.reshape(-1).astype(F32)])
    widths = [n_qkv, DN_W, SG_W, SG_W, d, d]
    out_shape = [jax.ShapeDtypeStruct((t, n), BF16) for n in widths]
    out_shape += [jax.ShapeDtypeStruct((n_ab, t), F32)]
    out_specs = [pl.BlockSpec((tm, n), lambda i: (i, 0)) for n in widths]
    out_specs += [pl.BlockSpec((n_ab, tm), lambda i: (0, i))]
    consts = [norm_w.reshape(1, d).astype(F32), w_qkv, conv_w.astype(F32), *w_rest, w_ab.T,
              alog16.reshape(n_ab, 1), dtb16.reshape(n_ab, 1)]
    n_tiles = t // tm
    x_specs = [pl.BlockSpec((tm, d), lambda i: (0, 0)),
               pl.BlockSpec((tm, d), lambda i: (jnp.minimum(i + 1, n_tiles - 1), 0)),
               pl.BlockSpec((CONV_HALO_ROWS, d), lambda i: (jnp.maximum(i * hb - 1, 0), 0)),
               pl.BlockSpec((CONV_HALO_ROWS, d), lambda i: (jnp.minimum((i + 1) * hb, last_halo), 0))]
    return pl.pallas_call(
        functools.partial(_inproj_kernel, tiles_per_seq=seq // tm),
        grid=(n_tiles,),
        in_specs=x_specs + [_const_spec(c.shape) for c in consts],
        out_specs=out_specs,
        out_shape=out_shape,
        scratch_shapes=[pltpu.VMEM((tm + 2 * CONV_HALO_ROWS, n_qkv), F32),
                        pltpu.VMEM((tm + 2 * CONV_HALO_ROWS, d), BF16), pltpu.VMEM((tm, d), BF16)],
        compiler_params=_params("arbitrary"),
        name="inproj",
    )(x2d, x2d, x2d, x2d, *consts)


def _delta_kernel(qf_ref, gtf_ref, qb_ref, gtb_ref, of_ref, ob_ref, state_ref, *, cb):
    c = DN_CHUNK
    nc = cb // c
    qkv_refs = (qf_ref, qb_ref)
    n_gate = gtf_ref.shape[0]

    @pl.when(pl.program_id(1) == 0)
    def _():
        state_ref[...] = jnp.zeros_like(state_ref)

    ri = lax.broadcasted_iota(jnp.int32, (LANES, LANES), 0)
    ci = lax.broadcasted_iota(jnp.int32, (LANES, LANES), 1)
    same = (ri & -c) == (ci & -c)
    ge = jnp.where(same & (ri >= ci), 1.0, 0.0)
    le = jnp.where(same & (ri <= ci), 1.0, 0.0)
    gates_t = (gtf_ref[...], gtb_ref[...])
    gc_t = (_cumsum_lanes(gates_t[0], le) * LOG2_E, _cumsum_lanes(gates_t[1], ge) * LOG2_E)
    is_beta = lax.broadcasted_iota(jnp.int32, (n_gate, cb), 0) < 2 * DN_HEADS
    pad_rows = jnp.zeros((LANES - n_gate, cb), F32)
    gc = tuple(jnp.concatenate([jnp.where(is_beta, gates_t[d], gc_t[d]), pad_rows], axis=0).T for d in range(2))

    ri2 = lax.broadcasted_iota(jnp.int32, (c, 2 * c), 0)
    lane2 = lax.broadcasted_iota(jnp.int32, (c, 2 * c), 1)
    left = lane2 < c
    eye2 = jnp.where(ri2 == (lane2 & (c - 1)), 1.0, 0.0)
    ri1 = lax.broadcasted_iota(jnp.int32, (c, c), 0)
    ci1 = lax.broadcasted_iota(jnp.int32, (c, c), 1)
    incl = (ri1 >= ci1, ri1 <= ci1)
    strict = (ri1 > ci1, ri1 < ci1)
    zeros1 = jnp.zeros((c, c), F32)

    state = {(d, h): state_ref[d, h] for d in range(2) for h in range(DN_HEADS)}
    steps_done = [0]
    o_refs = (of_ref, ob_ref)

    def operand(col0, it):
        d, n, h = it
        return qkv_refs[d][n * c:(n + 1) * c, col0 + h * DN_D:col0 + (h + 1) * DN_D]

    def group(s):
        items = [(d, (nc - 1 - s) if d else s, h) for d in range(2) for h in range(DN_HEADS)]
        q = functools.partial(operand, 0)
        k = functools.partial(operand, DN_W)
        v = functools.partial(operand, 2 * DN_W)
        beta, g_col, g_last, decay = {}, {}, {}, {}
        qkk = {it: _dot_nt(jnp.concatenate([q(it), k(it)], axis=0), k(it)) for it in items}
        yield
        for it in items:
            d, n, h = it
            rows = slice(n * c, (n + 1) * c)
            bcol = d * DN_HEADS + h
            gcol = (2 + d) * DN_HEADS + h
            beta[it] = jnp.broadcast_to(gc[d][rows, bcol:bcol + 1], (c, DN_D))
            g_col[it] = jnp.broadcast_to(gc[d][rows, gcol:gcol + 1], (c, DN_D))
            g_row = gc_t[d][gcol:gcol + 1, rows]
            g_last[it] = g_col[it][0:1, :] if d else g_col[it][c - 1:c, :]
            decay[it] = jnp.exp2(g_col[it][:, :c] - g_row)
        yield
        a_intra = {it: jnp.where(incl[it[0]], qkk[it][:c] * decay[it], 0.0) for it in items}
        lm = {it: jnp.where(strict[it[0]], beta[it][:, :c] * qkk[it][c:] * decay[it], 0.0) for it in items}
        lm_l = {it: jnp.concatenate([lm[it], zeros1], axis=1) for it in items}
        sq = {it: _dot(lm[it], lm_l[it]) for it in items}
        yield
        z = {it: jnp.where(left, sq[it], eye2 - pltpu.roll(lm_l[it], c, 1)) for it in items}
        eg = {it: jnp.exp2(g_col[it]) for it in items}
        uw_rhs = {it: jnp.concatenate([(v(it).astype(F32) * beta[it]).astype(BF16),
                                       (k(it).astype(F32) * (beta[it] * eg[it])).astype(BF16)], axis=1)
                  for it in items}
        power = 2
        while power < c:
            zb = {it: z[it].astype(BF16) for it in items}
            r = {it: _dot(zb[it][:, :c], zb[it]) for it in items}
            yield
            keep = ~left if 2 * power < c else True
            z = {it: jnp.where(keep, z[it], 0.0) + r[it] for it in items}
            power *= 2
        uw = {it: _dot(pltpu.roll(z[it], c, 1)[:, :c], uw_rhs[it]) for it in items}
        kga = {it: jnp.concatenate([(k(it).astype(F32) * jnp.exp2(g_last[it] - g_col[it])).T.astype(BF16),
                                    a_intra[it].astype(BF16)], axis=0) for it in items}
        yield
        assert steps_done[0] == s, "recurrence steps must be issued in order"
        wq = {it: jnp.concatenate([uw[it][:, DN_D:], q(it).astype(F32) * eg[it]], axis=0) for it in items}
        ws_qs = {it: _dot(wq[it], state[it[0], it[2]]) for it in items}
        yield
        v_new = {it: uw[it][:, :DN_D] - ws_qs[it][:c] for it in items}
        upd = {it: _dot(kga[it], v_new[it]) for it in items}
        yield
        for it in items:
            d, n, h = it
            state[d, h] = state[d, h] * jnp.exp2(g_last[it]) + upd[it][:DN_D]
            o = ws_qs[it][c:] + upd[it][DN_D:]
            o_refs[d][n * c:(n + 1) * c, h * DN_D:(h + 1) * DN_D] = o.astype(BF16)
        steps_done[0] += 1

    _run_staggered([group(s) for s in range(nc)], DELTA_GROUP_STAGGER)
    for (d, h), val in state.items():
        state_ref[d, h] = val


def _delta_scan(qkv, gates_t, cb):
    bsz, seq, n_qkv = qkv.shape
    nb = seq // cb
    n_gate = gates_t.shape[0]

    def stream_specs(blk):
        return [
            pl.BlockSpec((None, cb, n_qkv), lambda b, i: (b, blk(i), 0)),
            pl.BlockSpec((n_gate, cb), lambda b, i: (0, b * nb + blk(i))),
        ]

    fwd = lambda i: i
    bwd = lambda i: nb - 1 - i
    stream = (qkv, gates_t)
    return pl.pallas_call(
        functools.partial(_delta_kernel, cb=cb),
        grid=(bsz, nb),
        in_specs=stream_specs(fwd) + stream_specs(bwd),
        out_specs=[pl.BlockSpec((None, cb, DN_W), lambda b, i: (b, fwd(i), 0)),
                   pl.BlockSpec((None, cb, DN_W), lambda b, i: (b, bwd(i), 0))],
        out_shape=[jax.ShapeDtypeStruct((bsz, seq, DN_W), BF16)] * 2,
        scratch_shapes=[pltpu.VMEM((2, DN_HEADS, DN_D, DN_D), F32)],
        compiler_params=_params("parallel", "arbitrary"),
        name="delta",
    )(*stream, *stream)


def _mix_kernel(x_ref, of_ref, ob_ref, dng_ref, sgu_ref, sgv_ref, ga_ref, gb_ref,
                dnw_ref, wua_ref, lnw_ref, lnb_ref, ws_ref, bst_ref, wub_ref, wout_ref,
                y_ref, dn_s, sg_s, mg_s):
    tm = x_ref.shape[0]
    for n in range(tm // SG_CHUNK):
        rows = slice(n * SG_CHUNK, (n + 1) * SG_CHUNK)
        for h in range(DN_HEADS):
            cols = slice(h * DN_D, (h + 1) * DN_D)
            o = of_ref[rows, cols].astype(F32) + ob_ref[rows, cols].astype(F32)
            dn_s[rows, cols] = (_rms(o, dnw_ref[...]) * _silu(dng_ref[rows, cols].astype(F32))).astype(BF16)
        v = _gelu_tanh(sgv_ref[rows, :].astype(F32))
        vc = v - jnp.mean(v, -1, keepdims=True)
        v = (vc * lax.rsqrt(jnp.mean(vc * vc, -1, keepdims=True) + EPS) * lnw_ref[...] + lnb_ref[...]).astype(BF16)
        for g in range(SG_GROUPS):
            cols = slice(g * SG_CHUNK, (g + 1) * SG_CHUNK)
            mixed = _dot(ws_ref[g], v[:, cols]) + bst_ref[:, g:g + 1]
            sg_s[rows, cols] = (_gelu_tanh(sgu_ref[rows, cols].astype(F32)) * mixed).astype(BF16)
    y_a = _dot(dn_s[...], wua_ref[...])
    y_b = _dot(sg_s[...], wub_ref[...])
    for n in range(tm // SG_CHUNK):
        rows = slice(n * SG_CHUNK, (n + 1) * SG_CHUNK)
        mg_s[rows, :] = _sigmoid_mix(ga_ref[rows, :].astype(F32), y_a[rows], gb_ref[rows, :].astype(F32),
                                     y_b[rows]).astype(BF16)
    y_ref[...] = x_ref[...] + _dot(mg_s[...], wout_ref[...])


def _mix(x2d, o_f, o_b, dng, sgu, sgv, ga, gb, dn_norm_w, w_up_a, ln_w, ln_b, sg_w, sg_b, w_up_b, w_out, tm):
    t, d = x2d.shape
    consts = [dn_norm_w.reshape(1, DN_D).astype(F32), w_up_a.astype(BF16),
              ln_w.reshape(1, SG_W).astype(F32), ln_b.reshape(1, SG_W).astype(F32),
              sg_w.astype(BF16), sg_b.T.astype(F32), w_up_b.astype(BF16), w_out.astype(BF16)]
    acts = [x2d, o_f, o_b, dng, sgu, sgv, ga, gb]
    return pl.pallas_call(
        _mix_kernel,
        grid=(t // tm,),
        in_specs=[pl.BlockSpec((tm, a.shape[1]), lambda i: (i, 0)) for a in acts]
                 + [_const_spec(c.shape) for c in consts],
        out_specs=pl.BlockSpec((tm, d), lambda i: (i, 0)),
        out_shape=jax.ShapeDtypeStruct((t, d), F32),
        scratch_shapes=[pltpu.VMEM((tm, DN_W), BF16), pltpu.VMEM((tm, SG_W), BF16), pltpu.VMEM((tm, d), BF16)],
        compiler_params=_params("parallel"),
        name="mix",
    )(*acts, *consts)


def _norm_proj_kernel(x_ref, nw_ref, w_ref, o_ref):
    o_ref[...] = _dot(_rms(x_ref[...], nw_ref[...]), w_ref[...]).astype(o_ref.dtype)


def _norm_proj(x2d, norm_w, w, tm):
    t, d = x2d.shape
    n = w.shape[1]
    return pl.pallas_call(
        _norm_proj_kernel,
        grid=(t // tm,),
        in_specs=[pl.BlockSpec((tm, d), lambda i: (i, 0)), _const_spec((1, d)), _const_spec(w.shape)],
        out_specs=pl.BlockSpec((tm, n), lambda i: (i, 0)),
        out_shape=jax.ShapeDtypeStruct((t, n), BF16),
        compiler_params=_params("parallel"),
        name="mem_kv",
    )(x2d, norm_w.reshape(1, d).astype(F32), w.astype(BF16))


def _xattn_kernel(x_ref, kv_ref, nw_ref, wq_ref, wo_ref, y_ref, o_s):
    d = x_ref.shape[1]
    hd = d // XA_HEADS
    x = x_ref[...]
    q = _dot(_rms(x, nw_ref[...]), wq_ref[...])
    for h in range(XA_HEADS):
        cols = slice(h * hd, (h + 1) * hd)
        s = _dot_nt(q[:, cols], kv_ref[:, cols]) * (hd ** -0.5)
        p = jnp.exp(s - jnp.max(s, -1, keepdims=True))
        p = p / jnp.sum(p, -1, keepdims=True)
        o_s[:, cols] = _dot(p, kv_ref[:, d + h * hd:d + (h + 1) * hd]).astype(BF16)
    y_ref[...] = x + _dot(o_s[...], wo_ref[...])


def _xattn(x3d, kv, norm_w, w_q, w_o, tm):
    bsz, seq, d = x3d.shape
    n_mem = kv.shape[1]
    consts = [norm_w.reshape(1, d).astype(F32), w_q.astype(BF16), w_o.astype(BF16)]
    return pl.pallas_call(
        _xattn_kernel,
        grid=(bsz, seq // tm),
        in_specs=[pl.BlockSpec((None, tm, d), lambda b, i: (b, i, 0)),
                  pl.BlockSpec((None, n_mem, 2 * d), lambda b, i: (b, 0, 0))]
                 + [_const_spec(c.shape) for c in consts],
        out_specs=pl.BlockSpec((None, tm, d), lambda b, i: (b, i, 0)),
        out_shape=jax.ShapeDtypeStruct((bsz, seq, d), F32),
        scratch_shapes=[pltpu.VMEM((tm, d), BF16)],
        compiler_params=_params("parallel", "parallel"),
        name="xattn",
    )(x3d, kv, *consts)


def _ffn_kernel(x_ref, nw_ref, wg_ref, wu_ref, wd_ref, fw_ref, y_ref, *, ff_chunks):
    x = x_ref[...]
    h = _rms(x, nw_ref[...]).astype(BF16)
    acc = x
    for lo, hi in ff_chunks:
        a = _silu(_dot(h, wg_ref[:, lo:hi])) * _dot(h, wu_ref[:, lo:hi])
        acc = acc + _dot(a, wd_ref[lo:hi, :])
    y_ref[...] = _rms(acc, fw_ref[...])


def _ffn(x2d, norm_w, w_gate_up, w_down, final_w, tm):
    t, d = x2d.shape
    d_ff = w_down.shape[0]
    assert d_ff % MXU_COLS == 0
    ff_step = 3 * MXU_COLS
    ff_chunks = tuple((lo, min(lo + ff_step, d_ff)) for lo in range(0, d_ff, ff_step))
    consts = [norm_w.reshape(1, d).astype(F32), w_gate_up[:, :d_ff].astype(BF16),
              w_gate_up[:, d_ff:].astype(BF16), w_down.astype(BF16), final_w.reshape(1, d).astype(F32)]
    return pl.pallas_call(
        functools.partial(_ffn_kernel, ff_chunks=ff_chunks),
        grid=(t // tm,),
        in_specs=[pl.BlockSpec((tm, d), lambda i: (i, 0))] + [_const_spec(c.shape) for c in consts],
        out_specs=pl.BlockSpec((tm, d), lambda i: (i, 0)),
        out_shape=jax.ShapeDtypeStruct((t, d), F32),
        compiler_params=_params("parallel"),
        name="ffn",
    )(x2d, *consts)


def _largest_tile(n, cap):
    t = cap
    while n % t:
        t //= 2
    return t


def _trunk(x, mem, norm_mix_w, w_in, conv_w, dn_a_log, dn_dt_bias, dn_norm_w, w_up_a, sg_ln_w, sg_ln_b,
           sg_w, sg_b, w_up_b, w_out, norm_xa_w, norm_mem_w, xa_w_q, xa_w_kv, xa_w_o, norm_ffn_w,
           ffn_w_gate_up, ffn_w_down, final_norm_w):
    bsz, seq, d = x.shape
    t = bsz * seq
    assert seq % SG_CHUNK == 0 and seq % DN_CHUNK == 0 and LANES % DN_CHUNK == 0
    tm = _largest_tile(seq, ROW_TILE)
    cb = _largest_tile(seq, DELTA_BLOCK)
    x2d = x.reshape(t, d)
    qkv, dng, sgu, sgv, ga, gb, gates_t = _inproj(x2d, seq, norm_mix_w, w_in, conv_w, dn_a_log, dn_dt_bias, tm)
    o_f, o_b = _delta_scan(qkv.reshape(bsz, seq, -1), gates_t, cb)
    n_mem = mem.shape[1]
    kv = _norm_proj(mem.reshape(bsz * n_mem, d), norm_mem_w, xa_w_kv, _largest_tile(bsz * n_mem, MXU_COLS))
    x2d = _mix(x2d, o_f.reshape(t, DN_W), o_b.reshape(t, DN_W), dng, sgu, sgv, ga, gb, dn_norm_w, w_up_a,
               sg_ln_w, sg_ln_b, sg_w, sg_b, w_up_b, w_out, tm)
    x3d = _xattn(x2d.reshape(bsz, seq, d), kv.reshape(bsz, n_mem, 2 * d), norm_xa_w, xa_w_q, xa_w_o, tm)
    y = _ffn(x3d.reshape(t, d), norm_ffn_w, ffn_w_gate_up, ffn_w_down, final_norm_w, _largest_tile(seq, FFN_TILE))
    return y.reshape(bsz, seq, d)


def kernel(x_prompt, x_sample, mem_prompt, mem_sample, norm_mix_w, w_in, conv_w, dn_a_log, dn_dt_bias, dn_norm_w, w_up_a, sg_ln_w, sg_ln_b, sg_w, sg_b, w_up_b, w_out, norm_xa_w, norm_mem_w, xa_w_q, xa_w_kv, xa_w_o, norm_ffn_w, ffn_w_gate_up, ffn_w_down, final_norm_w):
    depth = w_in.shape[0]
    assert depth == 1, "the FFN kernel fuses the final norm, which assumes a single layer"
    layer = (norm_mix_w[0], w_in[0], conv_w[0], dn_a_log[0], dn_dt_bias[0], dn_norm_w[0], w_up_a[0],
             sg_ln_w[0], sg_ln_b[0], sg_w[0], sg_b[0], w_up_b[0], w_out[0], norm_xa_w[0], norm_mem_w[0],
             xa_w_q[0], xa_w_kv[0], xa_w_o[0], norm_ffn_w[0], ffn_w_gate_up[0], ffn_w_down[0], final_norm_w)
    y_prompt = _trunk(x_prompt, mem_prompt, *layer)
    y_sample = _trunk(x_sample, mem_sample, *layer)
    return (y_prompt, y_sample)
```
